```python
import jax, jax.numpy as jnp
from jax import lax
import numpy as np

D_MODEL = 1024
BATCH = 2
SEQ = 16384
DEPTH = 1

HEAD_DIM = 64
N_ATTN_HEADS = 16
D_ATTN = N_ATTN_HEADS * HEAD_DIM
N_RWKV_HEADS = D_MODEL // HEAD_DIM
D_RWKV = N_RWKV_HEADS * HEAD_DIM
ROPE_DIM = HEAD_DIM // 4
ROPE_THETA = 500000.0
DILATED_GROUPS = ((128, 1), (512, 4), (2048, 16))
BLOCK = 128
DECAY_LORA = 64
ICLR_LORA = 64
GATE_LORA = 128
D_FF = -(-8 * D_MODEL // (3 * 256)) * 256
RMS_EPS = 1e-6
GN_EPS = 64e-5
N_BRANCHES = 2
D_SHIFTED = 3 * D_RWKV + DECAY_LORA + ICLR_LORA + GATE_LORA
D_IN_PROJ = 3 * D_ATTN + D_SHIFTED + N_BRANCHES * D_MODEL

kernel_name = "hybrid_dilated_attn_rwkv7_block"


def _rmsnorm(x, g):
    xf = x.astype(jnp.float32)
    y = xf * lax.rsqrt(jnp.mean(xf * xf, axis=-1, keepdims=True) + RMS_EPS)
    return (y * g.astype(jnp.float32)).astype(x.dtype)


def _partial_rotary(t, pos):
    half = ROPE_DIM // 2
    inv_freq = ROPE_THETA ** (-jnp.arange(half, dtype=jnp.float32) * (2.0 / ROPE_DIM))
    ang = pos.astype(jnp.float32)[:, None] * inv_freq[None, :]
    cos = jnp.cos(ang)[None, :, None, :]
    sin = jnp.sin(ang)[None, :, None, :]
    t1, t2, rest = t[..., :half], t[..., half:ROPE_DIM], t[..., ROPE_DIM:]
    return jnp.concatenate([t1 * cos - t2 * sin, t2 * cos + t1 * sin, rest], axis=-1)


def _dilated_window_attention(q, k, v, window, dilation):
    B, S, H, Dh = q.shape
    span = window // dilation
    chunk = dilation * BLOCK
    L = -(-S // chunk) * chunk
    n_blk = L // chunk
    pad = ((0, 0), (0, L - S), (0, 0), (0, 0))

    def to_blocks(t):
        t = jnp.pad(t, pad).reshape(B, L // dilation, dilation, H, Dh)
        t = jnp.swapaxes(t, 1, 2)
        return t.reshape(B, dilation, n_blk, BLOCK, H, Dh)

    def with_prev(t):
        prev = jnp.concatenate([jnp.zeros_like(t[:, :, :1]), t[:, :, :-1]], axis=2)
        return jnp.concatenate([prev, t], axis=3)

    qb = to_blocks(q)
    kw = with_prev(to_blocks(k))
    vw = with_prev(to_blocks(v))
    s = jnp.einsum('brnqhd,brnkhd->brnhqk', qb, kw) * (Dh ** -0.5)
    qi = jnp.arange(BLOCK)[:, None]
    kj = jnp.arange(2 * BLOCK)[None, :]
    dist = qi + BLOCK - kj
    band = (dist >= 0) & (dist <= span)
    first = (jnp.arange(n_blk)[:, None, None] > 0) | (kj[None] >= BLOCK)
    mask = (band[None] & first)[:, None]
    s = jnp.where(mask, s, -jnp.inf)
    lse = jax.nn.logsumexp(s, axis=-1)
    p = jnp.exp(s - lse[..., None])
    o = jnp.einsum('brnhqk,brnkhd->brnqhd', p, vw)

    def from_blocks(t):
        t = t.reshape(B, dilation, L // dilation, *t.shape[4:])
        t = jnp.swapaxes(t, 1, 2)
        return t.reshape(B, L, *t.shape[3:])[:, :S]

    return from_blocks(o), from_blocks(jnp.swapaxes(lse, 3, 4))


def _attention_branch(qkv, pos):
    B, S, _ = qkv.shape
    q, k, v = jnp.split(qkv.astype(jnp.float32), 3, axis=-1)
    q = _partial_rotary(q.reshape(B, S, N_ATTN_HEADS, HEAD_DIM), pos)
    k = _partial_rotary(k.reshape(B, S, N_ATTN_HEADS, HEAD_DIM), pos)
    v = v.reshape(B, S, N_ATTN_HEADS, HEAD_DIM)
    outs, lses = [], []
    for window, dilation in DILATED_GROUPS:
        o_g, lse_g = _dilated_window_attention(q, k, v, window, dilation)
        outs.append(o_g)
        lses.append(lse_g)
    wts = jax.nn.softmax(jnp.stack(lses, axis=0), axis=0)
    o = jnp.sum(wts[..., None] * jnp.stack(outs, axis=0), axis=0)
    return o.reshape(B, S, D_ATTN)


def _rwkv7_step(state, inp):
    r_t, w_t, k_t, v_t, a_t, b_t = inp
    sa = jnp.einsum('bhvk,bhk->bhv', state, a_t)
    state = (state * w_t[:, :, None, :] + sa[..., None] * b_t[:, :, None, :]
             + v_t[..., None] * k_t[:, :, None, :])
    y = jnp.einsum('bhvk,bhk->bhv', state, r_t)
    return state, y


def _rwkv7_branch(cols, mu, w0, w2, a0, a2, g2, k_k, k_a, r_k, ln_w, ln_b):
    B, S, _ = cols.shape
    f32 = jnp.float32
    prev = jnp.pad(cols, ((0, 0), (1, 0), (0, 0)))[:, :-1]
    xm = cols + (prev - cols) * mu
    splits = [D_RWKV, 2 * D_RWKV, 3 * D_RWKV, 3 * D_RWKV + DECAY_LORA,
              3 * D_RWKV + DECAY_LORA + ICLR_LORA]
    r, k, v, w_lo, a_lo, g_lo = jnp.split(xm, splits, axis=-1)
    w_log = -jax.nn.softplus(-(w0 + jnp.tanh(w_lo) @ w2).astype(f32)) - 0.5
    decay = jnp.exp(-jnp.exp(w_log))
    a = jax.nn.sigmoid((a0 + a_lo @ a2).astype(f32))
    g = (jax.nn.sigmoid(g_lo) @ g2).astype(f32)

    def heads(t):
        return t.astype(f32).reshape(B, S, N_RWKV_HEADS, HEAD_DIM)

    r, k, v, decay, a = heads(r), heads(k), heads(v), heads(decay), heads(a)
    hk = (N_RWKV_HEADS, HEAD_DIM)
    kk = k * k_k.astype(f32).reshape(hk)
    kk = kk / jnp.maximum(jnp.sqrt(jnp.sum(kk * kk, axis=-1, keepdims=True)), 1e-12)
    k = k * (1.0 + (a - 1.0) * k_a.astype(f32).reshape(hk))

    def tm(t):
        return jnp.swapaxes(t, 0, 1)

    state0 = jnp.zeros((B, N_RWKV_HEADS, HEAD_DIM, HEAD_DIM), f32)
    _, y = lax.scan(_rwkv7_step, state0,
                    (tm(r), tm(decay), tm(k), tm(v), tm(-kk), tm(kk * a)))
    y = jnp.swapaxes(y, 0, 1)
    mean = jnp.mean(y, axis=-1, keepdims=True)
    var = jnp.mean(jnp.square(y - mean), axis=-1, keepdims=True)
    yn = ((y - mean) * lax.rsqrt(var + GN_EPS) * ln_w.astype(f32).reshape(hk)
          + ln_b.astype(f32).reshape(hk))
    bonus = jnp.sum(r * k * r_k.astype(f32), axis=-1, keepdims=True) * v
    return (yn + bonus).reshape(B, S, D_RWKV) * g


def setup_inputs(seed: int = 0) -> dict:
    key = jax.random.key(seed)
    ks = jax.random.split(key, 22)
    f32 = jnp.float32
    L = DEPTH

    def nrm(k, shape, scale):
        return jax.random.normal(k, shape, f32) * scale

    return {
        "x": nrm(ks[0], (BATCH, SEQ, D_MODEL), 1.0),
        "norm_mix_g": 1.0 + nrm(ks[1], (L, D_MODEL), 0.02),
        "w_in": nrm(ks[2], (L, D_MODEL, D_IN_PROJ), D_MODEL ** -0.5),
        "shift_mu": jax.random.uniform(ks[3], (L, D_SHIFTED), f32),
        "decay_w0": jax.random.uniform(ks[4], (L, D_RWKV), f32, -6.0, 1.0),
        "decay_w2": nrm(ks[5], (L, DECAY_LORA, D_RWKV), 0.1 * DECAY_LORA ** -0.5),
        "iclr_a0": nrm(ks[6], (L, D_RWKV), 0.1),
        "iclr_a2": nrm(ks[7], (L, ICLR_LORA, D_RWKV), 0.1 * ICLR_LORA ** -0.5),
        "gate_g2": nrm(ks[8], (L, GATE_LORA, D_RWKV), GATE_LORA ** -0.5),
        "k_k": 0.85 + nrm(ks[9], (L, D_RWKV), 0.02),
        "k_a": 1.0 + nrm(ks[10], (L, D_RWKV), 0.02),
        "r_k": nrm(ks[11], (L, N_RWKV_HEADS, HEAD_DIM), 0.1),
        "ln_x_w": 1.0 + nrm(ks[12], (L, D_RWKV), 0.02),
        "ln_x_b": nrm(ks[13], (L, D_RWKV), 0.02),
        "proj_attn": nrm(ks[14], (L, D_ATTN, D_MODEL), D_ATTN ** -0.5),
        "proj_rwkv": nrm(ks[15], (L, D_RWKV, D_MODEL), D_RWKV ** -0.5),
        "w_out": nrm(ks[16], (L, D_MODEL, D_MODEL), D_MODEL ** -0.5),
        "norm_ffn_g": 1.0 + nrm(ks[17], (L, D_MODEL), 0.02),
        "ffn_w_gate": nrm(ks[18], (L, D_MODEL, D_FF), D_MODEL ** -0.5),
        "ffn_w_up": nrm(ks[19], (L, D_MODEL, D_FF), D_MODEL ** -0.5),
        "ffn_w_down": nrm(ks[20], (L, D_FF, D_MODEL), D_FF ** -0.5),
        "norm_final_g": 1.0 + nrm(ks[21], (D_MODEL,), 0.02),
    }


def reference(x, norm_mix_g, w_in, shift_mu, decay_w0, decay_w2, iclr_a0, iclr_a2,
              gate_g2, k_k, k_a, r_k, ln_x_w, ln_x_b, proj_attn, proj_rwkv, w_out,
              norm_ffn_g, ffn_w_gate, ffn_w_up, ffn_w_down, norm_final_g):
    B, S, _ = x.shape
    pos = jnp.arange(S, dtype=jnp.int32)
    for l in range(DEPTH):
        h = _rmsnorm(x, norm_mix_g[l])
        p = h @ w_in[l]
        qkv = p[..., :3 * D_ATTN]
        shifted = p[..., 3 * D_ATTN:3 * D_ATTN + D_SHIFTED]
        gate_logits = p[..., 3 * D_ATTN + D_SHIFTED:]
        o_a = _attention_branch(qkv, pos)
        o_b = _rwkv7_branch(shifted, shift_mu[l], decay_w0[l], decay_w2[l], iclr_a0[l],
                            iclr_a2[l], gate_g2[l], k_k[l], k_a[l], r_k[l],
                            ln_x_w[l], ln_x_b[l])
        gates = jax.nn.sigmoid(gate_logits.astype(jnp.float32))
        g_a, g_b = gates[..., :D_MODEL], gates[..., D_MODEL:]
        merged = (g_a * (o_a.astype(x.dtype) @ proj_attn[l]).astype(jnp.float32)
                  + g_b * (o_b.astype(x.dtype) @ proj_rwkv[l]).astype(jnp.float32))
        x = x + merged.astype(x.dtype) @ w_out[l]
        h = _rmsnorm(x, norm_ffn_g[l])
        x = x + (jax.nn.silu(h @ ffn_w_gate[l]) * (h @ ffn_w_up[l])) @ ffn_w_down[l]
    return _rmsnorm(x, norm_final_g)
```

```python
import functools

import jax
import jax.numpy as jnp
from jax import lax
from jax.experimental import pallas as pl
from jax.experimental.pallas import tpu as pltpu

F32 = jnp.float32
BF16 = jnp.bfloat16

D_MODEL = 1024
HEAD_DIM = 64
N_HEADS = 16
ROPE_DIM = 16
ROPE_THETA = 500000.0
DILATED_GROUPS = ((128, 1), (512, 4), (2048, 16))
ATTN_BLOCK = 128
DECAY_LORA = 64
ICLR_LORA = 64
GATE_LORA = 128
D_FF = 2816
RMS_EPS = 1e-6
GN_EPS = 64e-5
D_SHIFTED = 3 * D_MODEL + DECAY_LORA + ICLR_LORA + GATE_LORA

LANES = 128
N_PAIRS = D_MODEL // LANES
CHUNK = 64
GROUP = 256

NT_DIMS = (((1,), (1,)), ((), ()))


def _dot(a, b):
    return jnp.dot(a, b, preferred_element_type=F32)


def _dot_nt(a, b):
    return lax.dot_general(a, b, NT_DIMS, preferred_element_type=F32)


def _rmsnorm(x, g):
    return x * lax.rsqrt(jnp.mean(x * x, axis=-1, keepdims=True) + RMS_EPS) * g


def _qkv_kernel(x_ref, g_ref, w_ref, c_ref, s1_ref, s2_ref, *refs, tm):
    outs, slab_ref = refs[:-1], refs[-1]
    h = _rmsnorm(x_ref[0], g_ref[...]).astype(BF16)
    p = _dot(h, w_ref[...])
    c, s1, s2 = c_ref[...], s1_ref[...], s2_ref[...]

    def rot(t):
        return t * c + pltpu.roll(t, 8, 1) * s1 + pltpu.roll(t, LANES - 8, 1) * s2

    for blk in range(N_PAIRS):
        lo = blk * LANES
        slab_ref[blk] = rot(p[:, lo:lo + LANES]) * (HEAD_DIM ** -0.5)
        slab_ref[N_PAIRS + blk] = rot(p[:, D_MODEL + lo:D_MODEL + lo + LANES])
        slab_ref[2 * N_PAIRS + blk] = p[:, 2 * D_MODEL + lo:2 * D_MODEL + lo + LANES]

    for gi, (_, d) in enumerate(DILATED_GROUPS):
        rows = tm // d
        for j in range(3):
            out = outs[3 * gi + j]
            for blk in range(N_PAIRS):
                slab = j * N_PAIRS + blk
                for r in range(d):
                    src = slab_ref[slab] if d == 1 else slab_ref[slab, pl.ds(r, rows, stride=d), :]
                    out[0, r, :, blk * LANES:(blk + 1) * LANES] = src.astype(BF16)


def _qkv_call(x, g, w, c, s1, s2, tm=256):
    b, s, _ = x.shape
    const = lambda bb, i: (0, 0)
    pos = lambda bb, i: (i, 0)
    out_specs, out_shape = [], []
    for _, d in DILATED_GROUPS:
        for _j in range(3):
            out_specs.append(pl.BlockSpec((1, d, tm // d, D_MODEL), lambda bb, i: (bb, 0, i, 0)))
            out_shape.append(jax.ShapeDtypeStruct((b, d, s // d, D_MODEL), BF16))
    return pl.pallas_call(
        functools.partial(_qkv_kernel, tm=tm),
        grid=(b, s // tm),
        in_specs=[pl.BlockSpec((1, tm, D_MODEL), lambda bb, i: (bb, i, 0)),
                  pl.BlockSpec((1, D_MODEL), const),
                  pl.BlockSpec((D_MODEL, 3 * D_MODEL), const),
                  pl.BlockSpec((tm, LANES), pos), pl.BlockSpec((tm, LANES), pos),
                  pl.BlockSpec((tm, LANES), pos)],
        out_specs=out_specs,
        out_shape=out_shape,
        scratch_shapes=[pltpu.VMEM((3 * N_PAIRS, tm, LANES), F32)],
        compiler_params=pltpu.CompilerParams(dimension_semantics=("arbitrary", "arbitrary")),
        name="qkv_proj",
    )(x, g, w, c, s1, s2)


def _cols_kernel(x_ref, g_ref, w_ref, o_ref):
    h = _rmsnorm(x_ref[...], g_ref[...]).astype(BF16)
    o_ref[...] = _dot(h, w_ref[...])


def _cols_call(x2, g, w, tm=256):
    t = x2.shape[0]
    n = w.shape[1]
    return pl.pallas_call(
        _cols_kernel,
        grid=(t // tm,),
        in_specs=[pl.BlockSpec((tm, D_MODEL), lambda i: (i, 0)),
                  pl.BlockSpec((1, D_MODEL), lambda i: (0, 0)),
                  pl.BlockSpec((D_MODEL, n), lambda i: (0, 0))],
        out_specs=pl.BlockSpec((tm, n), lambda i: (i, 0)),
        out_shape=jax.ShapeDtypeStruct((t, n), F32),
        compiler_params=pltpu.CompilerParams(dimension_semantics=("arbitrary",)),
        name="rwkv_proj",
    )(x2, g, w)


def _attn_kernel(q_ref, kc_ref, kp_ref, vc_ref, vp_ref, o_ref, lse_ref, *, tq):
    n = pl.program_id(2)
    blk = ATTN_BLOCK
    qi = lax.broadcasted_iota(jnp.int32, (blk, 1), 0)
    kj = lax.broadcasted_iota(jnp.int32, (1, 2 * blk), 1)
    band = (kj >= qi) & (kj <= qi + blk)
    band_first = band & ((kj >= blk) | (n > 0))
    lane = lax.broadcasted_iota(jnp.int32, (1, LANES), 1)
    head0 = lane < HEAD_DIM

    for i in range(tq // blk):
        r0 = i * blk
        mask = band_first if i == 0 else band
        lse_tile = jnp.zeros((blk, LANES), F32)
        for hp in range(N_PAIRS):
            l0 = hp * LANES
            qs = q_ref[r0:r0 + blk, l0:l0 + LANES]
            if i == 0:
                kprev = kp_ref[:, l0:l0 + LANES]
                vprev = vp_ref[:, l0:l0 + LANES]
            else:
                kprev = kc_ref[r0 - blk:r0, l0:l0 + LANES]
                vprev = vc_ref[r0 - blk:r0, l0:l0 + LANES]
            k2 = jnp.concatenate([kprev, kc_ref[r0:r0 + blk, l0:l0 + LANES]], axis=0)
            v2 = jnp.concatenate([vprev, vc_ref[r0:r0 + blk, l0:l0 + LANES]], axis=0)
            zq = jnp.zeros_like(qs)
            q_st = jnp.concatenate([jnp.where(head0, qs, zq), jnp.where(head0, zq, qs)], axis=0)
            s = _dot_nt(q_st, k2)
            ps, ms, ls = [], [], []
            for hh in range(2):
                sh = jnp.where(mask, s[hh * blk:(hh + 1) * blk], -jnp.inf)
                m = jnp.max(sh, axis=1, keepdims=True)
                p = jnp.exp(sh - m)
                ps.append(p.astype(BF16))
                ms.append(m)
                ls.append(jnp.sum(p, axis=1, keepdims=True))
            o2 = _dot(jnp.concatenate(ps, axis=0), v2)
            o = jnp.where(head0, o2[:blk] / ls[0], o2[blk:] / ls[1])
            o_ref[r0:r0 + blk, l0:l0 + LANES] = o.astype(BF16)
            for hh in range(2):
                lse_tile = jnp.where(lane == 2 * hp + hh, ms[hh] + jnp.log(ls[hh]), lse_tile)
        lse_ref[r0:r0 + blk, :] = lse_tile


def _attn_call(q, k, v, tq=256):
    b, d, sub, _ = q.shape
    cur = lambda bb, r, n: (bb, r, n, 0)
    prev = lambda bb, r, n: (bb, r, jnp.maximum(n * (tq // ATTN_BLOCK) - 1, 0), 0)
    return pl.pallas_call(
        functools.partial(_attn_kernel, tq=tq),
        grid=(b, d, sub // tq),
        in_specs=[pl.BlockSpec((None, None, tq, D_MODEL), cur),
                  pl.BlockSpec((None, None, tq, D_MODEL), cur),
                  pl.BlockSpec((None, None, ATTN_BLOCK, D_MODEL), prev),
                  pl.BlockSpec((None, None, tq, D_MODEL), cur),
                  pl.BlockSpec((None, None, ATTN_BLOCK, D_MODEL), prev)],
        out_specs=[pl.BlockSpec((None, None, tq, D_MODEL), cur),
                   pl.BlockSpec((None, None, tq, LANES), cur)],
        out_shape=[jax.ShapeDtypeStruct((b, d, sub, D_MODEL), BF16),
                   jax.ShapeDtypeStruct((b, d, sub, LANES), F32)],
        compiler_params=pltpu.CompilerParams(
            dimension_semantics=("arbitrary", "arbitrary", "arbitrary")),
        name=f"dilated_attn_d{d}",
    )(q, k, k, v, v)


def _rwkv_kernel(cols_ref, mu_ref, w0_ref, w2a2_ref, a0_ref, g2_ref, kk_ref, ka_ref, rk_ref,
                 lnw_ref, lnb_ref, ones_ref, o_ref,
                 carry_ref, h_ref, at_ref, rt_ref, bt_ref, kt_ref, bh_ref, kh_ref, v_ref, y_ref,
                 pc_ref):
    c_len = CHUNK

    @pl.when(pl.program_id(1) == 0)
    def _():
        carry_ref[...] = jnp.zeros_like(carry_ref)
        h_ref[...] = jnp.zeros_like(h_ref)

    row = lax.broadcasted_iota(jnp.int32, (c_len, 1), 0)
    lane = lax.broadcasted_iota(jnp.int32, (1, LANES), 1)
    head0 = lane < HEAD_DIM
    ones_bd = ones_ref[...]

    def headsum(x):
        parts = []
        for gi in range(D_MODEL // GROUP):
            xg = x[:, gi * GROUP:(gi + 1) * GROUP]
            hi = xg.astype(BF16)
            lo = (xg - hi.astype(F32)).astype(BF16)
            parts.append(_dot(hi, ones_bd) + _dot(lo, ones_bd))
        return jnp.concatenate(parts, axis=1)

    cols = cols_ref[0]
    prev = jnp.where(row == 0, carry_ref[...], pltpu.roll(cols, 1, 0))
    carry_ref[...] = cols[c_len - 1:c_len, :]
    xm = cols + (prev - cols) * mu_ref[...]
    r = xm[:, 0:D_MODEL]
    k = xm[:, D_MODEL:2 * D_MODEL]
    v = xm[:, 2 * D_MODEL:3 * D_MODEL]
    slab = xm[:, 3 * D_MODEL:3 * D_MODEL + LANES]
    g_lo = xm[:, 3 * D_MODEL + LANES:]
    z = jnp.where(head0, jnp.tanh(slab), slab).astype(BF16)
    lora = _dot(z, w2a2_ref[...])
    w_log = -jax.nn.softplus(-(w0_ref[...] + lora[:, :D_MODEL])) - 0.5
    lw = -jnp.exp(w_log)
    eta = jax.nn.sigmoid(a0_ref[...] + lora[:, D_MODEL:])
    gate = _dot(jax.nn.sigmoid(g_lo).astype(BF16), g2_ref[...])
    kk = k * kk_ref[...]
    kk = kk / jnp.maximum(jnp.sqrt(headsum(kk * kk)), 1e-12)
    k_mod = k * (1.0 + (eta - 1.0) * ka_ref[...])
    bonus = headsum(r * k_mod * rk_ref[...]) * v
    a_s = -kk
    b_s = kk * eta

    cl = lw
    for sh in (1, 2, 4, 8, 16, 32):
        cl = cl + jnp.where(row >= sh, pltpu.roll(cl, sh, 0), 0.0)
    cl_end = cl[c_len - 1:c_len, :]
    e_cl = jnp.exp(cl)
    e_ncl = jnp.exp(-cl)
    e_end = jnp.exp(cl_end - cl)
    a_t = (a_s * jnp.exp(cl - lw)).astype(BF16)
    r_t = r * e_cl
    b_t = (b_s * e_ncl).astype(BF16)
    k_t = (k_mod * e_ncl).astype(BF16)
    b_h = b_s * e_end
    k_h = k_mod * e_end
    v_b = v.astype(BF16)
    pc = jnp.exp(cl_end)
    for p in range(N_PAIRS):
        sl = slice(p * LANES, (p + 1) * LANES)
        at_ref[p] = a_t[:, sl]
        rt_ref[p] = r_t[:, sl]
        bt_ref[p] = b_t[:, sl]
        kt_ref[p] = k_t[:, sl]
        bh_ref[p] = b_h[:, sl]
        kh_ref[p] = k_h[:, sl]
        v_ref[p] = v_b[:, sl]
        pc_ref[p] = pc[:, sl]

    col = lane % HEAD_DIM
    strict = row > col
    incl = row >= col
    eye = (row == col).astype(F32)

    def bd(y):
        zz = jnp.zeros_like(y)
        return jnp.concatenate([jnp.where(head0, y, zz), jnp.where(head0, zz, y)], axis=0)

    def pair_body(p, _):
        at, rt, bt, kt, vb = at_ref[p], rt_ref[p], bt_ref[p], kt_ref[p], v_ref[p]
        a_all = _dot_nt(jnp.concatenate([at, rt.astype(BF16)], axis=0),
                        jnp.concatenate([bd(bt), bd(kt)], axis=0))
        l_ab = jnp.where(strict, a_all[:c_len, :LANES], 0.0)
        a_ak = jnp.where(strict, a_all[:c_len, LANES:], 0.0)
        a_rb = jnp.where(incl, a_all[c_len:, :LANES], 0.0)
        a_rk = jnp.where(incl, a_all[c_len:, LANES:], 0.0)

        l_b = l_ab.astype(BF16)
        s_acc = eye + l_ab
        q_pow = _dot(l_b, bd(l_b))
        for _i in range(4):
            q_b = q_pow.astype(BF16)
            res = _dot(jnp.concatenate([s_acc.astype(BF16), q_b], axis=0), bd(q_b))
            s_acc = s_acc + res[:c_len]
            q_pow = res[c_len:]
        s_acc = s_acc + _dot(s_acc.astype(BF16), bd(q_pow.astype(BF16)))
        t_b = s_acc.astype(BF16)

        res = _dot(jnp.concatenate([a_ak.astype(BF16), a_rk.astype(BF16)], axis=0), bd(vb))
        x2, yv2 = res[:c_len], res[c_len:]
        a_hat = _dot(t_b, bd(at))
        u_v = _dot(t_b, bd(x2.astype(BF16)))
        a_hat_b, u_v_b = a_hat.astype(BF16), u_v.astype(BF16)
        a_rb_b = a_rb.astype(BF16)
        r_hat = rt + _dot(a_rb_b, bd(a_hat_b))
        y_v = _dot(a_rb_b, bd(u_v_b)) + yv2

        lt = jnp.concatenate([bh_ref[p], kh_ref[p]], axis=0).T.astype(BF16)
        rhs = jnp.concatenate(
            [jnp.concatenate([a_hat_b, jnp.zeros_like(a_hat_b)], axis=0),
             jnp.concatenate([u_v_b, vb], axis=0)], axis=1)
        mn = _dot(lt, rhs)
        m_p = jnp.where(head0, mn[:c_len, :LANES], mn[c_len:, :LANES]) + eye * pc_ref[p]
        n_p = jnp.where(head0, mn[:c_len, LANES:], mn[c_len:, LANES:])

        res = _dot(jnp.concatenate([m_p.astype(BF16), r_hat.astype(BF16)], axis=0),
                   bd(h_ref[p].astype(BF16)))
        h_ref[p] = res[:c_len] + n_p
        y_ref[p] = res[c_len:] + y_v
        return 0

    lax.fori_loop(0, N_PAIRS, pair_body, 0)

    y = jnp.concatenate([y_ref[p] for p in range(N_PAIRS)], axis=1)
    mean = headsum(y) * (1.0 / HEAD_DIM)
    yc = y - mean
    var = headsum(yc * yc) * (1.0 / HEAD_DIM)
    yn = yc * lax.rsqrt(var + GN_EPS) * lnw_ref[...] + lnb_ref[...]
    o_ref[0] = ((yn + bonus) * gate).astype(BF16)


def _rwkv_call(cols, mu, w0, w2a2, a0, g2, k_k, k_a, r_k, ln_w, ln_b):
    b, s, _ = cols.shape
    ones_bd = jnp.kron(jnp.eye(GROUP // HEAD_DIM, dtype=F32),
                       jnp.ones((HEAD_DIM, HEAD_DIM), F32)).astype(BF16)
    const = lambda bb, t: (0, 0)
    vec = pl.BlockSpec((1, D_MODEL), const)
    pair_f32 = pltpu.VMEM((N_PAIRS, CHUNK, LANES), F32)
    pair_bf16 = pltpu.VMEM((N_PAIRS, CHUNK, LANES), BF16)
    return pl.pallas_call(
        _rwkv_kernel,
        grid=(b, s // CHUNK),
        in_specs=[pl.BlockSpec((1, CHUNK, D_SHIFTED), lambda bb, t: (bb, t, 0)),
                  pl.BlockSpec((1, D_SHIFTED), const),
                  vec,
                  pl.BlockSpec((LANES, 2 * D_MODEL), const),
                  vec,
                  pl.BlockSpec((GATE_LORA, D_MODEL), const),
                  vec, vec, vec, vec, vec,
                  pl.BlockSpec((GROUP, GROUP), const)],
        out_specs=pl.BlockSpec((1, CHUNK, D_MODEL), lambda bb, t: (bb, t, 0)),
        out_shape=jax.ShapeDtypeStruct((b, s, D_MODEL), BF16),
        scratch_shapes=[pltpu.VMEM((1, D_SHIFTED), F32),
                        pair_f32,
                        pair_bf16, pair_f32, pair_bf16, pair_bf16, pair_f32, pair_f32,
                        pair_bf16, pair_f32,
                        pltpu.VMEM((N_PAIRS, 1, LANES), F32)],
        compiler_params=pltpu.CompilerParams(dimension_semantics=("arbitrary", "arbitrary")),
        name="rwkv7_mixer",
    )(cols, mu, w0, w2a2, a0, g2, k_k, k_a, r_k, ln_w, ln_b, ones_bd)


def _merge_kernel(x_ref, g_ref, wg_ref, o1_ref, o4_ref, o16_ref, l1_ref, l4_ref, l16_ref,
                  ob_ref, pa_ref, pb_ref, wo_ref, e_ref, out_ref, o_scr, lse_scr):
    x = x_ref[0]
    h = _rmsnorm(x, g_ref[...]).astype(BF16)
    gl = _dot(h, wg_ref[...])
    g_a = jax.nn.sigmoid(gl[:, :D_MODEL])
    g_b = jax.nn.sigmoid(gl[:, D_MODEL:])

    def natural_order(src_ref, dst_ref):
        _, d, rows, width = src_ref.shape
        for r in range(d):
            for cblk in range(width // LANES):
                val = src_ref[0, r, :, cblk * LANES:(cblk + 1) * LANES].astype(F32)
                if d == 1:
                    dst_ref[cblk] = val
                else:
                    dst_ref[cblk, pl.ds(r, rows, stride=d), :] = val

    lses = []
    for gi, l_ref in enumerate((l1_ref, l4_ref, l16_ref)):
        natural_order(l_ref, lse_scr.at[gi])
        lses.append(lse_scr[gi, 0])
    mx = jnp.maximum(jnp.maximum(lses[0], lses[1]), lses[2])
    es = [jnp.exp(l - mx) for l in lses]
    den = es[0] + es[1] + es[2]
    expand = e_ref[...]
    o_a = jnp.zeros(x.shape, F32)
    for e, o_ref in zip(es, (o1_ref, o4_ref, o16_ref)):
        w = e / den
        hi = w.astype(BF16)
        lo = (w - hi.astype(F32)).astype(BF16)
        natural_order(o_ref, o_scr)
        o_g = jnp.concatenate([o_scr[p] for p in range(N_PAIRS)], axis=1)
        o_a = o_a + (_dot(hi, expand) + _dot(lo, expand)) * o_g

    merged = (g_a * _dot(o_a.astype(BF16), pa_ref[...])
              + g_b * _dot(ob_ref[0], pb_ref[...]))
    out_ref[0] = x + _dot(merged.astype(BF16), wo_ref[...])


def _merge_call(x, g, wg, os_, lses, ob, pa, pb, wo, tm=256):
    b, s, _ = x.shape
    expand = jnp.kron(jnp.eye(N_HEADS, dtype=F32), jnp.ones((1, HEAD_DIM), F32))
    expand = jnp.concatenate([expand, jnp.zeros((LANES - N_HEADS, D_MODEL), F32)], 0).astype(BF16)
    const = lambda bb, i: (0, 0)
    wide = pl.BlockSpec((1, tm, D_MODEL), lambda bb, i: (bb, i, 0))
    sq = pl.BlockSpec((D_MODEL, D_MODEL), const)
    res = lambda a: pl.BlockSpec((1, a.shape[1], tm // a.shape[1], a.shape[3]),
                                 lambda bb, i: (bb, 0, i, 0))
    return pl.pallas_call(
        _merge_kernel,
        grid=(b, s // tm),
        in_specs=[wide, pl.BlockSpec((1, D_MODEL), const),
                  pl.BlockSpec((D_MODEL, 2 * D_MODEL), const),
                  *[res(a) for a in os_], *[res(a) for a in lses], wide, sq, sq, sq,
                  pl.BlockSpec((LANES, D_MODEL), const)],
        out_specs=wide,
        out_shape=jax.ShapeDtypeStruct((b, s, D_MODEL), F32),
        scratch_shapes=[pltpu.VMEM((N_PAIRS, tm, LANES), F32),
                        pltpu.VMEM((len(lses), 1, tm, LANES), F32)],
        compiler_params=pltpu.CompilerParams(dimension_semantics=("arbitrary", "arbitrary")),
        name="merge_proj",
    )(x, g, wg, *os_, *lses, ob, pa, pb, wo, expand)


def _ffn_kernel(x_ref, g_ref, wg_ref, wu_ref, wd_ref, gf_ref, out_ref, *, final_norm):
    x = x_ref[...]
    h = _rmsnorm(x, g_ref[...]).astype(BF16)
    act = (jax.nn.silu(_dot(h, wg_ref[...])) * _dot(h, wu_ref[...])).astype(BF16)
    x2 = x + _dot(act, wd_ref[...])
    out_ref[...] = _rmsnorm(x2, gf_ref[...]) if final_norm else x2


def _ffn_call(x2, g, wg, wu, wd, gf, final_norm, tm=256):
    t = x2.shape[0]
    row = lambda i: (i, 0)
    const = lambda i: (0, 0)
    vec = pl.BlockSpec((1, D_MODEL), const)
    return pl.pallas_call(
        functools.partial(_ffn_kernel, final_norm=final_norm),
        grid=(t // tm,),
        in_specs=[pl.BlockSpec((tm, D_MODEL), row), vec,
                  pl.BlockSpec((D_MODEL, D_FF), const), pl.BlockSpec((D_MODEL, D_FF), const),
                  pl.BlockSpec((D_FF, D_MODEL), const), vec],
        out_specs=pl.BlockSpec((tm, D_MODEL), row),
        out_shape=jax.ShapeDtypeStruct((t, D_MODEL), F32),
        compiler_params=pltpu.CompilerParams(dimension_semantics=("arbitrary",)),
        name="ffn_final",
    )(x2, g, wg, wu, wd, gf)


def _rotary_tables(seq):
    half = ROPE_DIM // 2
    inv_freq = ROPE_THETA ** (-jnp.arange(half, dtype=F32) * (2.0 / ROPE_DIM))
    ang = jnp.arange(seq, dtype=jnp.int32).astype(F32)[:, None] * inv_freq[None, :]
    cos, sin = jnp.cos(ang), jnp.sin(ang)
    zeros = lambda n: jnp.zeros((seq, n), F32)
    rest = HEAD_DIM - ROPE_DIM
    per_head = lambda parts: jnp.tile(jnp.concatenate(parts, axis=1), (1, LANES // HEAD_DIM))
    c = per_head([cos, cos, jnp.ones((seq, rest), F32)])
    s1 = per_head([zeros(half), sin, zeros(rest)])
    s2 = per_head([-sin, zeros(half), zeros(rest)])
    return c, s1, s2


def kernel(x, norm_mix_g, w_in, shift_mu, decay_w0, decay_w2, iclr_a0, iclr_a2, gate_g2, k_k, k_a, r_k, ln_x_w, ln_x_b, proj_attn, proj_rwkv, w_out, norm_ffn_g, ffn_w_gate, ffn_w_up, ffn_w_down, norm_final_g):
    b, s, d = x.shape
    assert d == D_MODEL and s % (16 * 256) == 0
    depth = w_in.shape[0]
    c, s1, s2 = _rotary_tables(s)
    vec = lambda a: a.reshape(1, -1).astype(F32)
    n_attn = 3 * D_MODEL
    for l in range(depth):
        g_mix = vec(norm_mix_g[l])
        w = w_in[l]
        qkv = _qkv_call(x, g_mix, w[:, :n_attn].astype(BF16), c, s1, s2)
        cols = _cols_call(x.reshape(b * s, d), g_mix,
                          w[:, n_attn:n_attn + D_SHIFTED].astype(BF16))
        os_, lses = [], []
        for gi in range(len(DILATED_GROUPS)):
            o_g, lse_g = _attn_call(*qkv[3 * gi:3 * gi + 3])
            os_.append(o_g)
            lses.append(lse_g)
        zero = jnp.zeros((DECAY_LORA, D_MODEL), F32)
        w2a2 = jnp.concatenate(
            [jnp.concatenate([decay_w2[l], zero], axis=1),
             jnp.concatenate([zero, iclr_a2[l]], axis=1)], axis=0).astype(BF16)
        o_b = _rwkv_call(cols.reshape(b, s, D_SHIFTED), vec(shift_mu[l]), vec(decay_w0[l]), w2a2,
                         vec(iclr_a0[l]), gate_g2[l].astype(BF16), vec(k_k[l]), vec(k_a[l]),
                         vec(r_k[l]), vec(ln_x_w[l]), vec(ln_x_b[l]))
        x = _merge_call(x, g_mix, w[:, n_attn + D_SHIFTED:].astype(BF16), os_, lses, o_b,
                        proj_attn[l].astype(BF16), proj_rwkv[l].astype(BF16),
                        w_out[l].astype(BF16))
        x = _ffn_call(x.reshape(b * s, d), vec(norm_ffn_g[l]), ffn_w_gate[l].astype(BF16),
                      ffn_w_up[l].astype(BF16), ffn_w_down[l].astype(BF16),
                      vec(norm_final_g), final_norm=(l == depth - 1)).reshape(b, s, d)
    return x
```

```python
import functools

import jax
import jax.numpy as jnp
from jax import lax
from jax.experimental import pallas as pl
from jax.experimental.pallas import tpu as pltpu

F32 = jnp.float32
BF16 = jnp.bfloat16

D_MODEL = 1024
HEAD_DIM = 64
N_HEADS = 16
ROPE_DIM = 16
ROPE_THETA = 500000.0
DILATED_GROUPS = ((128, 1), (512, 4), (2048, 16))
ATTN_BLOCK = 128
DECAY_LORA = 64
ICLR_LORA = 64
GATE_LORA = 128
D_FF = 2816
RMS_EPS = 1e-6
GN_EPS = 64e-5
D_SHIFTED = 3 * D_MODEL + DECAY_LORA + ICLR_LORA + GATE_LORA

LANES = 128
N_PAIRS = D_MODEL // LANES
CHUNK = 64
GROUP = 256

NT_DIMS = (((1,), (1,)), ((), ()))


def _dot(a, b):
    return jnp.dot(a, b, preferred_element_type=F32)


def _dot_nt(a, b):
    return lax.dot_general(a, b, NT_DIMS, preferred_element_type=F32)


def _rmsnorm(x, g):
    return x * lax.rsqrt(jnp.mean(x * x, axis=-1, keepdims=True) + RMS_EPS) * g


def _qkv_kernel(x_ref, g_ref, w_ref, c_ref, s1_ref, s2_ref, *refs, tm):
    outs, slab_ref = refs[:-1], refs[-1]
    h = _rmsnorm(x_ref[0], g_ref[...]).astype(BF16)
    p = _dot(h, w_ref[...])
    c, s1, s2 = c_ref[...], s1_ref[...], s2_ref[...]

    def rot(t):
        return t * c + pltpu.roll(t, 8, 1) * s1 + pltpu.roll(t, LANES - 8, 1) * s2

    for blk in range(N_PAIRS):
        lo = blk * LANES
        slab_ref[blk] = rot(p[:, lo:lo + LANES]) * (HEAD_DIM ** -0.5)
        slab_ref[N_PAIRS + blk] = rot(p[:, D_MODEL + lo:D_MODEL + lo + LANES])
        slab_ref[2 * N_PAIRS + blk] = p[:, 2 * D_MODEL + lo:2 * D_MODEL + lo + LANES]

    for gi, (_, d) in enumerate(DILATED_GROUPS):
        rows = tm // d
        for j in range(3):
            out = outs[3 * gi + j]
            for blk in range(N_PAIRS):
                slab = j * N_PAIRS + blk
                for r in range(d):
                    src = slab_ref[slab] if d == 1 else slab_ref[slab, pl.ds(r, rows, stride=d), :]
                    out[0, r, :, blk * LANES:(blk + 1) * LANES] = src.astype(BF16)


def _qkv_call(x, g, w, c, s1, s2, tm=256):
    b, s, _ = x.shape
    const = lambda bb, i: (0, 0)
    pos = lambda bb, i: (i, 0)
    out_specs, out_shape = [], []
    for _, d in DILATED_GROUPS:
        for _j in range(3):
            out_specs.append(pl.BlockSpec((1, d, tm // d, D_MODEL), lambda bb, i: (bb, 0, i, 0)))
            out_shape.append(jax.ShapeDtypeStruct((b, d, s // d, D_MODEL), BF16))
    return pl.pallas_call(
        functools.partial(_qkv_kernel, tm=tm),
        grid=(b, s // tm),
        in_specs=[pl.BlockSpec((1, tm, D_MODEL), lambda bb, i: (bb, i, 0)),
                  pl.BlockSpec((1, D_MODEL), const),
                  pl.BlockSpec((D_MODEL, 3 * D_MODEL), const),
                  pl.BlockSpec((tm, LANES), pos), pl.BlockSpec((tm, LANES), pos),
                  pl.BlockSpec((tm, LANES), pos)],
        out_specs=out_specs,
        out_shape=out_shape,
        scratch_shapes=[pltpu.VMEM((3 * N_PAIRS, tm, LANES), F32)],
        compiler_params=pltpu.CompilerParams(dimension_semantics=("arbitrary", "arbitrary")),
        name="qkv_proj",
    )(x, g, w, c, s1, s2)


def _cols_kernel(x_ref, g_ref, w_ref, o_ref):
    h = _rmsnorm(x_ref[...], g_ref[...]).astype(BF16)
    o_ref[...] = _dot(h, w_ref[...])


def _cols_call(x2, g, w, tm=256):
    t = x2.shape[0]
    n = w.shape[1]
    return pl.pallas_call(
        _cols_kernel,
        grid=(t // tm,),
        in_specs=[pl.BlockSpec((tm, D_MODEL), lambda i: (i, 0)),
                  pl.BlockSpec((1, D_MODEL), lambda i: (0, 0)),
                  pl.BlockSpec((D_MODEL, n), lambda i: (0, 0))],
        out_specs=pl.BlockSpec((tm, n), lambda i: (i, 0)),
        out_shape=jax.ShapeDtypeStruct((t, n), F32),
        compiler_params=pltpu.CompilerParams(dimension_semantics=("arbitrary",)),
        name="rwkv_proj",
    )(x2, g, w)


def _attn_kernel(q_ref, kc_ref, kp_ref, vc_ref, vp_ref, o_ref, lse_ref, *, tq):
    n = pl.program_id(2)
    blk = ATTN_BLOCK
    qi = lax.broadcasted_iota(jnp.int32, (blk, 1), 0)
    kj = lax.broadcasted_iota(jnp.int32, (1, 2 * blk), 1)
    band = (kj >= qi) & (kj <= qi + blk)
    band_first = band & ((kj >= blk) | (n > 0))
    lane = lax.broadcasted_iota(jnp.int32, (1, LANES), 1)
    head0 = lane < HEAD_DIM

    for i in range(tq // blk):
        r0 = i * blk
        mask = band_first if i == 0 else band
        lse_tile = jnp.zeros((blk, LANES), F32)
        for hp in range(N_PAIRS):
            l0 = hp * LANES
            qs = q_ref[r0:r0 + blk, l0:l0 + LANES]
            if i == 0:
                kprev = kp_ref[:, l0:l0 + LANES]
                vprev = vp_ref[:, l0:l0 + LANES]
            else:
                kprev = kc_ref[r0 - blk:r0, l0:l0 + LANES]
                vprev = vc_ref[r0 - blk:r0, l0:l0 + LANES]
            k2 = jnp.concatenate([kprev, kc_ref[r0:r0 + blk, l0:l0 + LANES]], axis=0)
            v2 = jnp.concatenate([vprev, vc_ref[r0:r0 + blk, l0:l0 + LANES]], axis=0)
            zq = jnp.zeros_like(qs)
            q_st = jnp.concatenate([jnp.where(head0, qs, zq), jnp.where(head0, zq, qs)], axis=0)
            s = _dot_nt(q_st, k2)
            ps, ms, ls = [], [], []
            for hh in range(2):
                sh = jnp.where(mask, s[hh * blk:(hh + 1) * blk], -jnp.inf)
                m = jnp.max(sh, axis=1, keepdims=True)
                p = jnp.exp(sh - m)
                ps.append(p.astype(BF16))
                ms.append(m)
                ls.append(jnp.sum(p, axis=1, keepdims=True))
            o2 = _dot(jnp.concatenate(ps, axis=0), v2)
            o = jnp.where(head0, o2[:blk] / ls[0], o2[blk:] / ls[1])
            o_ref[r0:r0 + blk, l0:l0 + LANES] = o.astype(BF16)
            for hh in range(2):
                lse_tile = jnp.where(lane == 2 * hp + hh, ms[hh] + jnp.log(ls[hh]), lse_tile)
        lse_ref[r0:r0 + blk, :] = lse_tile


def _attn_call(q, k, v, tq=256):
    b, d, sub, _ = q.shape
    cur = lambda bb, r, n: (bb, r, n, 0)
    prev = lambda bb, r, n: (bb, r, jnp.maximum(n * (tq // ATTN_BLOCK) - 1, 0), 0)
    return pl.pallas_call(
        functools.partial(_attn_kernel, tq=tq),
        grid=(b, d, sub // tq),
        in_specs=[pl.BlockSpec((None, None, tq, D_MODEL), cur),
                  pl.BlockSpec((None, None, tq, D_MODEL), cur),
                  pl.BlockSpec((None, None, ATTN_BLOCK, D_MODEL), prev),
                  pl.BlockSpec((None, None, tq, D_MODEL), cur),
                  pl.BlockSpec((None, None, ATTN_BLOCK, D_MODEL), prev)],
        out_specs=[pl.BlockSpec((None, None, tq, D_MODEL), cur),
                   pl.BlockSpec((None, None, tq, LANES), cur)],
        out_shape=[jax.ShapeDtypeStruct((b, d, sub, D_MODEL), BF16),
                   jax.ShapeDtypeStruct((b, d, sub, LANES), F32)],
        compiler_params=pltpu.CompilerParams(
            dimension_semantics=("arbitrary", "arbitrary", "arbitrary")),
        name=f"dilated_attn_d{d}",
    )(q, k, k, v, v)


def _rwkv_kernel(cols_ref, mu_ref, w0_ref, w2a2_ref, a0_ref, g2_ref, kk_ref, ka_ref, rk_ref,
                 lnw_ref, lnb_ref, ones_ref, o_ref, carry_ref, h_ref):
    c_len = CHUNK

    @pl.when(pl.program_id(1) == 0)
    def _():
        carry_ref[...] = jnp.zeros_like(carry_ref)
        h_ref[...] = jnp.zeros_like(h_ref)

    row = lax.broadcasted_iota(jnp.int32, (c_len, 1), 0)
    lane = lax.broadcasted_iota(jnp.int32, (1, LANES), 1)
    head0 = lane < HEAD_DIM
    ones_bd = ones_ref[...]

    def headsum(x):
        parts = []
        for gi in range(D_MODEL // GROUP):
            xg = x[:, gi * GROUP:(gi + 1) * GROUP]
            hi = xg.astype(BF16)
            lo = (xg - hi.astype(F32)).astype(BF16)
            parts.append(_dot(hi, ones_bd) + _dot(lo, ones_bd))
        return jnp.concatenate(parts, axis=1)

    cols = cols_ref[0]
    prev = jnp.where(row == 0, carry_ref[...], pltpu.roll(cols, 1, 0))
    carry_ref[...] = cols[c_len - 1:c_len, :]
    xm = cols + (prev - cols) * mu_ref[...]
    r = xm[:, 0:D_MODEL]
    k = xm[:, D_MODEL:2 * D_MODEL]
    v = xm[:, 2 * D_MODEL:3 * D_MODEL]
    slab = xm[:, 3 * D_MODEL:3 * D_MODEL + LANES]
    g_lo = xm[:, 3 * D_MODEL + LANES:]
    z = jnp.where(head0, jnp.tanh(slab), slab).astype(BF16)
    lora = _dot(z, w2a2_ref[...])
    w_log = -jax.nn.softplus(-(w0_ref[...] + lora[:, :D_MODEL])) - 0.5
    lw = -jnp.exp(w_log)
    eta = jax.nn.sigmoid(a0_ref[...] + lora[:, D_MODEL:])
    gate = _dot(jax.nn.sigmoid(g_lo).astype(BF16), g2_ref[...])
    kk = k * kk_ref[...]
    kk = kk / jnp.maximum(jnp.sqrt(headsum(kk * kk)), 1e-12)
    k_mod = k * (1.0 + (eta - 1.0) * ka_ref[...])
    bonus = headsum(r * k_mod * rk_ref[...]) * v
    a_s = -kk
    b_s = kk * eta

    cl = lw
    for sh in (1, 2, 4, 8, 16, 32):
        cl = cl + jnp.where(row >= sh, pltpu.roll(cl, sh, 0), 0.0)
    cl_end = cl[c_len - 1:c_len, :]
    e_cl = jnp.exp(cl)
    e_ncl = jnp.exp(-cl)
    e_end = jnp.exp(cl_end - cl)
    a_t = (a_s * jnp.exp(cl - lw)).astype(BF16)
    r_t = r * e_cl
    b_t = (b_s * e_ncl).astype(BF16)
    k_t = (k_mod * e_ncl).astype(BF16)
    b_h = b_s * e_end
    k_h = k_mod * e_end
    v_b = v.astype(BF16)
    pc = jnp.exp(cl_end)

    col = lane % HEAD_DIM
    strict = row > col
    incl = row >= col
    eye = (row == col).astype(F32)
    pairs = range(N_PAIRS)
    cut = lambda t: [t[:, p * LANES:(p + 1) * LANES] for p in pairs]
    at, rt, bt, kt, vb, bh, kh, pcs = (cut(t) for t in (a_t, r_t, b_t, k_t, v_b, b_h, k_h, pc))
    top = lambda ts: [t[:c_len] for t in ts]
    bot = lambda ts: [t[c_len:] for t in ts]
    stack = lambda xs, ys: [jnp.concatenate([x, y], axis=0) for x, y in zip(xs, ys)]
    to_bf16 = lambda ts: [t.astype(BF16) for t in ts]

    def bd(y):
        zz = jnp.zeros_like(y)
        return jnp.concatenate([jnp.where(head0, y, zz), jnp.where(head0, zz, y)], axis=0)

    def pmm(xs, ys):
        return [_dot(x, bd(y)) for x, y in zip(xs, ys)]

    a_all = [_dot_nt(lhs, jnp.concatenate([bd(b), bd(kk_)], axis=0))
             for lhs, b, kk_ in zip(stack(at, to_bf16(rt)), bt, kt)]
    l_ab = [jnp.where(strict, a[:c_len, :LANES], 0.0) for a in a_all]
    a_ak = [jnp.where(strict, a[:c_len, LANES:], 0.0).astype(BF16) for a in a_all]
    a_rb = [jnp.where(incl, a[c_len:, :LANES], 0.0).astype(BF16) for a in a_all]
    a_rk = [jnp.where(incl, a[c_len:, LANES:], 0.0).astype(BF16) for a in a_all]

    l_b = to_bf16(l_ab)
    s_acc = [eye + l for l in l_ab]
    q_b = to_bf16(pmm(l_b, l_b))
    for _i in range(4):
        res = pmm(stack(to_bf16(s_acc), q_b), q_b)
        s_acc = [s + r_ for s, r_ in zip(s_acc, top(res))]
        q_b = to_bf16(bot(res))
    s_acc = [s + r_ for s, r_ in zip(s_acc, pmm(to_bf16(s_acc), q_b))]
    t_b = to_bf16(s_acc)

    res = pmm(stack(a_ak, a_rk), vb)
    x2_b, yv2 = to_bf16(top(res)), bot(res)
    a_hat_b = to_bf16(pmm(t_b, at))
    u_v_b = to_bf16(pmm(t_b, x2_b))
    r_hat = [r_ + d for r_, d in zip(rt, pmm(a_rb, a_hat_b))]
    y_v = [d + y2 for d, y2 in zip(pmm(a_rb, u_v_b), yv2)]

    lts = [jnp.concatenate([b, kk_], axis=0).T.astype(BF16) for b, kk_ in zip(bh, kh)]
    mns = [_dot(lt, jnp.concatenate(
        [jnp.concatenate([ah, jnp.zeros_like(ah)], axis=0),
         jnp.concatenate([uv, v_], axis=0)], axis=1))
        for lt, ah, uv, v_ in zip(lts, a_hat_b, u_v_b, vb)]
    m_p = [jnp.where(head0, mn[:c_len, :LANES], mn[c_len:, :LANES]) + eye * pc_
           for mn, pc_ in zip(mns, pcs)]
    n_p = [jnp.where(head0, mn[:c_len, LANES:], mn[c_len:, LANES:]) for mn in mns]

    res = pmm(stack(to_bf16(m_p), to_bf16(r_hat)), [h_ref[p].astype(BF16) for p in pairs])
    for p in pairs:
        h_ref[p] = res[p][:c_len] + n_p[p]
    y = jnp.concatenate([r_[c_len:] + yv for r_, yv in zip(res, y_v)], axis=1)

    mean = headsum(y) * (1.0 / HEAD_DIM)
    yc = y - mean
    var = headsum(yc * yc) * (1.0 / HEAD_DIM)
    yn = yc * lax.rsqrt(var + GN_EPS) * lnw_ref[...] + lnb_ref[...]
    o_ref[0] = ((yn + bonus) * gate).astype(BF16)


def _rwkv_call(cols, mu, w0, w2a2, a0, g2, k_k, k_a, r_k, ln_w, ln_b):
    b, s, _ = cols.shape
    ones_bd = jnp.kron(jnp.eye(GROUP // HEAD_DIM, dtype=F32),
                       jnp.ones((HEAD_DIM, HEAD_DIM), F32)).astype(BF16)
    const = lambda bb, t: (0, 0)
    vec = pl.BlockSpec((1, D_MODEL), const)
    return pl.pallas_call(
        _rwkv_kernel,
        grid=(b, s // CHUNK),
        in_specs=[pl.BlockSpec((1, CHUNK, D_SHIFTED), lambda bb, t: (bb, t, 0)),
                  pl.BlockSpec((1, D_SHIFTED), const),
                  vec,
                  pl.BlockSpec((LANES, 2 * D_MODEL), const),
                  vec,
                  pl.BlockSpec((GATE_LORA, D_MODEL), const),
                  vec, vec, vec, vec, vec,
                  pl.BlockSpec((GROUP, GROUP), const)],
        out_specs=pl.BlockSpec((1, CHUNK, D_MODEL), lambda bb, t: (bb, t, 0)),
        out_shape=jax.ShapeDtypeStruct((b, s, D_MODEL), BF16),
        scratch_shapes=[pltpu.VMEM((1, D_SHIFTED), F32),
                        pltpu.VMEM((N_PAIRS, CHUNK, LANES), F32)],
        compiler_params=pltpu.CompilerParams(dimension_semantics=("arbitrary", "arbitrary")),
        name="rwkv7_mixer",
    )(cols, mu, w0, w2a2, a0, g2, k_k, k_a, r_k, ln_w, ln_b, ones_bd)


def _merge_kernel(x_ref, g_ref, wg_ref, o1_ref, o4_ref, o16_ref, l1_ref, l4_ref, l16_ref,
                  ob_ref, pa_ref, pb_ref, wo_ref, e_ref, out_ref, o_scr, lse_scr):
    x = x_ref[0]
    h = _rmsnorm(x, g_ref[...]).astype(BF16)
    gl = _dot(h, wg_ref[...])
    g_a = jax.nn.sigmoid(gl[:, :D_MODEL])
    g_b = jax.nn.sigmoid(gl[:, D_MODEL:])

    def natural_order(src_ref, dst_ref):
        _, d, rows, width = src_ref.shape
        for r in range(d):
            for cblk in range(width // LANES):
                val = src_ref[0, r, :, cblk * LANES:(cblk + 1) * LANES].astype(F32)
                if d == 1:
                    dst_ref[cblk] = val
                else:
                    dst_ref[cblk, pl.ds(r, rows, stride=d), :] = val

    lses = []
    for gi, l_ref in enumerate((l1_ref, l4_ref, l16_ref)):
        natural_order(l_ref, lse_scr.at[gi])
        lses.append(lse_scr[gi, 0])
    mx = jnp.maximum(jnp.maximum(lses[0], lses[1]), lses[2])
    es = [jnp.exp(l - mx) for l in lses]
    den = es[0] + es[1] + es[2]
    expand = e_ref[...]
    o_a = jnp.zeros(x.shape, F32)
    for e, o_ref in zip(es, (o1_ref, o4_ref, o16_ref)):
        w = e / den
        hi = w.astype(BF16)
        lo = (w - hi.astype(F32)).astype(BF16)
        natural_order(o_ref, o_scr)
        o_g = jnp.concatenate([o_scr[p] for p in range(N_PAIRS)], axis=1)
        o_a = o_a + (_dot(hi, expand) + _dot(lo, expand)) * o_g

    merged = (g_a * _dot(o_a.astype(BF16), pa_ref[...])
              + g_b * _dot(ob_ref[0], pb_ref[...]))
    out_ref[0] = x + _dot(merged.astype(BF16), wo_ref[...])


def _merge_call(x, g, wg, os_, lses, ob, pa, pb, wo, tm=256):
    b, s, _ = x.shape
    expand = jnp.kron(jnp.eye(N_HEADS, dtype=F32), jnp.ones((1, HEAD_DIM), F32))
    expand = jnp.concatenate([expand, jnp.zeros((LANES - N_HEADS, D_MODEL), F32)], 0).astype(BF16)
    const = lambda bb, i: (0, 0)
    wide = pl.BlockSpec((1, tm, D_MODEL), lambda bb, i: (bb, i, 0))
    sq = pl.BlockSpec((D_MODEL, D_MODEL), const)
    res = lambda a: pl.BlockSpec((1, a.shape[1], tm // a.shape[1], a.shape[3]),
                                 lambda bb, i: (bb, 0, i, 0))
    return pl.pallas_call(
        _merge_kernel,
        grid=(b, s // tm),
        in_specs=[wide, pl.BlockSpec((1, D_MODEL), const),
                  pl.BlockSpec((D_MODEL, 2 * D_MODEL), const),
                  *[res(a) for a in os_], *[res(a) for a in lses], wide, sq, sq, sq,
                  pl.BlockSpec((LANES, D_MODEL), const)],
        out_specs=wide,
        out_shape=jax.ShapeDtypeStruct((b, s, D_MODEL), F32),
        scratch_shapes=[pltpu.VMEM((N_PAIRS, tm, LANES), F32),
                        pltpu.VMEM((len(lses), 1, tm, LANES), F32)],
        compiler_params=pltpu.CompilerParams(dimension_semantics=("arbitrary", "arbitrary")),
        name="merge_proj",
    )(x, g, wg, *os_, *lses, ob, pa, pb, wo, expand)


def _ffn_kernel(x_ref, g_ref, wg_ref, wu_ref, wd_ref, gf_ref, out_ref, *, final_norm):
    x = x_ref[...]
    h = _rmsnorm(x, g_ref[...]).astype(BF16)
    act = (jax.nn.silu(_dot(h, wg_ref[...])) * _dot(h, wu_ref[...])).astype(BF16)
    x2 = x + _dot(act, wd_ref[...])
    out_ref[...] = _rmsnorm(x2, gf_ref[...]) if final_norm else x2


def _ffn_call(x2, g, wg, wu, wd, gf, final_norm, tm=256):
    t = x2.shape[0]
    row = lambda i: (i, 0)
    const = lambda i: (0, 0)
    vec = pl.BlockSpec((1, D_MODEL), const)
    return pl.pallas_call(
        functools.partial(_ffn_kernel, final_norm=final_norm),
        grid=(t // tm,),
        in_specs=[pl.BlockSpec((tm, D_MODEL), row), vec,
                  pl.BlockSpec((D_MODEL, D_FF), const), pl.BlockSpec((D_MODEL, D_FF), const),
                  pl.BlockSpec((D_FF, D_MODEL), const), vec],
        out_specs=pl.BlockSpec((tm, D_MODEL), row),
        out_shape=jax.ShapeDtypeStruct((t, D_MODEL), F32),
        compiler_params=pltpu.CompilerParams(dimension_semantics=("arbitrary",)),
        name="ffn_final",
    )(x2, g, wg, wu, wd, gf)


def _rotary_tables(seq):
    half = ROPE_DIM // 2
    inv_freq = ROPE_THETA ** (-jnp.arange(half, dtype=F32) * (2.0 / ROPE_DIM))
    ang = jnp.arange(seq, dtype=jnp.int32).astype(F32)[:, None] * inv_freq[None, :]
    cos, sin = jnp.cos(ang), jnp.sin(ang)
    zeros = lambda n: jnp.zeros((seq, n), F32)
    rest = HEAD_DIM - ROPE_DIM
    per_head = lambda parts: jnp.tile(jnp.concatenate(parts, axis=1), (1, LANES // HEAD_DIM))
    c = per_head([cos, cos, jnp.ones((seq, rest), F32)])
    s1 = per_head([zeros(half), sin, zeros(rest)])
    s2 = per_head([-sin, zeros(half), zeros(rest)])
    return c, s1, s2


def kernel(x, norm_mix_g, w_in, shift_mu, decay_w0, decay_w2, iclr_a0, iclr_a2, gate_g2, k_k, k_a, r_k, ln_x_w, ln_x_b, proj_attn, proj_rwkv, w_out, norm_ffn_g, ffn_w_gate, ffn_w_up, ffn_w_down, norm_final_g):
    b, s, d = x.shape
    assert d == D_MODEL and s % (16 * 256) == 0
    depth = w_in.shape[0]
    c, s1, s2 = _rotary_tables(s)
    vec = lambda a: a.reshape(1, -1).astype(F32)
    n_attn = 3 * D_MODEL
    for l in range(depth):
        g_mix = vec(norm_mix_g[l])
        w = w_in[l]
        qkv = _qkv_call(x, g_mix, w[:, :n_attn].astype(BF16), c, s1, s2)
        cols = _cols_call(x.reshape(b * s, d), g_mix,
                          w[:, n_attn:n_attn + D_SHIFTED].astype(BF16))
        os_, lses = [], []
        for gi in range(len(DILATED_GROUPS)):
            o_g, lse_g = _attn_call(*qkv[3 * gi:3 * gi + 3])
            os_.append(o_g)
            lses.append(lse_g)
        zero = jnp.zeros((DECAY_LORA, D_MODEL), F32)
        w2a2 = jnp.concatenate(
            [jnp.concatenate([decay_w2[l], zero], axis=1),
             jnp.concatenate([zero, iclr_a2[l]], axis=1)], axis=0).astype(BF16)
        o_b = _rwkv_call(cols.reshape(b, s, D_SHIFTED), vec(shift_mu[l]), vec(decay_w0[l]), w2a2,
                         vec(iclr_a0[l]), gate_g2[l].astype(BF16), vec(k_k[l]), vec(k_a[l]),
                         vec(r_k[l]), vec(ln_x_w[l]), vec(ln_x_b[l]))
        x = _merge_call(x, g_mix, w[:, n_attn + D_SHIFTED:].astype(BF16), os_, lses, o_b,
                        proj_attn[l].astype(BF16), proj_rwkv[l].astype(BF16),
                        w_out[l].astype(BF16))
        x = _ffn_call(x.reshape(b * s, d), vec(norm_ffn_g[l]), ffn_w_gate[l].astype(BF16),
                      ffn_w_up[l].astype(BF16), ffn_w_down[l].astype(BF16),
                      vec(norm_final_g), final_norm=(l == depth - 1)).reshape(b, s, d)
    return x
```

```python
import functools

import jax
import jax.numpy as jnp
from jax import lax
from jax.experimental import pallas as pl
from jax.experimental.pallas import tpu as pltpu

F32 = jnp.float32
BF16 = jnp.bfloat16

D_MODEL = 1024
HEAD_DIM = 64
N_HEADS = 16
ROPE_DIM = 16
ROPE_THETA = 500000.0
DILATED_GROUPS = ((128, 1), (512, 4), (2048, 16))
ATTN_BLOCK = 128
DECAY_LORA = 64
ICLR_LORA = 64
GATE_LORA = 128
D_FF = 2816
RMS_EPS = 1e-6
GN_EPS = 64e-5
D_SHIFTED = 3 * D_MODEL + DECAY_LORA + ICLR_LORA + GATE_LORA

LANES = 128
N_PAIRS = D_MODEL // LANES
CHUNK = 64
GROUP = 256

LOG2_E = 1.4426950408889634
NT_DIMS = (((1,), (1,)), ((), ()))


def _dot(a, b):
    return jnp.dot(a, b, preferred_element_type=F32)


def _dot_nt(a, b):
    return lax.dot_general(a, b, NT_DIMS, preferred_element_type=F32)


def _rmsnorm(x, g):
    return x * lax.rsqrt(jnp.mean(x * x, axis=-1, keepdims=True) + RMS_EPS) * g


def _qkv_kernel(x_ref, g_ref, w_ref, c_ref, s1_ref, s2_ref, *refs, tm):
    outs, slab_ref = refs[:-1], refs[-1]
    h = _rmsnorm(x_ref[0], g_ref[...]).astype(BF16)
    p = _dot(h, w_ref[...])
    c, s1, s2 = c_ref[...], s1_ref[...], s2_ref[...]

    def rot(t):
        return t * c + pltpu.roll(t, 8, 1) * s1 + pltpu.roll(t, LANES - 8, 1) * s2

    for blk in range(N_PAIRS):
        lo = blk * LANES
        slab_ref[blk] = rot(p[:, lo:lo + LANES]) * (HEAD_DIM ** -0.5 * LOG2_E)
        slab_ref[N_PAIRS + blk] = rot(p[:, D_MODEL + lo:D_MODEL + lo + LANES])
        slab_ref[2 * N_PAIRS + blk] = p[:, 2 * D_MODEL + lo:2 * D_MODEL + lo + LANES]

    for gi, (_, d) in enumerate(DILATED_GROUPS):
        rows = tm // d
        for j in range(3):
            out = outs[3 * gi + j]
            for blk in range(N_PAIRS):
                slab = j * N_PAIRS + blk
                for r in range(d):
                    src = slab_ref[slab] if d == 1 else slab_ref[slab, pl.ds(r, rows, stride=d), :]
                    out[0, r, :, blk * LANES:(blk + 1) * LANES] = src.astype(BF16)


def _qkv_call(x, g, w, c, s1, s2, tm=256):
    b, s, _ = x.shape
    const = lambda bb, i: (0, 0)
    pos = lambda bb, i: (i, 0)
    out_specs, out_shape = [], []
    for _, d in DILATED_GROUPS:
        for _j in range(3):
            out_specs.append(pl.BlockSpec((1, d, tm // d, D_MODEL), lambda bb, i: (bb, 0, i, 0)))
            out_shape.append(jax.ShapeDtypeStruct((b, d, s // d, D_MODEL), BF16))
    return pl.pallas_call(
        functools.partial(_qkv_kernel, tm=tm),
        grid=(b, s // tm),
        in_specs=[pl.BlockSpec((1, tm, D_MODEL), lambda bb, i: (bb, i, 0)),
                  pl.BlockSpec((1, D_MODEL), const),
                  pl.BlockSpec((D_MODEL, 3 * D_MODEL), const),
                  pl.BlockSpec((tm, LANES), pos), pl.BlockSpec((tm, LANES), pos),
                  pl.BlockSpec((tm, LANES), pos)],
        out_specs=out_specs,
        out_shape=out_shape,
        scratch_shapes=[pltpu.VMEM((3 * N_PAIRS, tm, LANES), F32)],
        compiler_params=pltpu.CompilerParams(dimension_semantics=("arbitrary", "arbitrary")),
        name="qkv_proj",
    )(x, g, w, c, s1, s2)


def _rwkv_feat_kernel(x_ref, g_ref, w_ref, mu_ref, w0_ref, w2a2_ref, a0_ref, g2_ref, kk_ref,
                      ka_ref, rk_ref, ones_ref,
                      at_ref, bt_ref, kt_ref, vb_ref, lt_ref, rt_ref, gate_ref, bonus_ref, pc_ref,
                      carry_ref, *, tm):
    @pl.when(pl.program_id(1) == 0)
    def _():
        carry_ref[...] = jnp.zeros_like(carry_ref)

    lane = lax.broadcasted_iota(jnp.int32, (1, LANES), 1)
    head0 = lane < HEAD_DIM
    ones_bd = ones_ref[...]

    h = _rmsnorm(x_ref[0], g_ref[...]).astype(BF16)
    cols = _dot(h, w_ref[...])
    row_t = lax.broadcasted_iota(jnp.int32, (tm, 1), 0)
    prev = jnp.where(row_t == 0, carry_ref[...], pltpu.roll(cols, 1, 0))
    carry_ref[...] = cols[tm - 1:tm, :]
    xm_all = cols + (prev - cols) * mu_ref[...]

    row = lax.broadcasted_iota(jnp.int32, (CHUNK, 1), 0)
    for ci in range(tm // CHUNK):
        rows = slice(ci * CHUNK, (ci + 1) * CHUNK)
        xm = xm_all[rows]
        r = xm[:, 0:D_MODEL]
        k = xm[:, D_MODEL:2 * D_MODEL]
        v = xm[:, 2 * D_MODEL:3 * D_MODEL]
        slab = xm[:, 3 * D_MODEL:3 * D_MODEL + LANES]
        g_lo = xm[:, 3 * D_MODEL + LANES:]
        z = jnp.where(head0, jnp.tanh(slab), slab).astype(BF16)
        lora = _dot(z, w2a2_ref[...])
        lw = jax.nn.sigmoid(w0_ref[...] + lora[:, :D_MODEL]) * (-(jnp.e ** -0.5))
        eta = jax.nn.sigmoid(a0_ref[...] + lora[:, D_MODEL:])
        gate_ref[0, rows, :] = _dot(jax.nn.sigmoid(g_lo).astype(BF16), g2_ref[...])
        kk = k * kk_ref[...]
        kk = kk * jnp.minimum(lax.rsqrt(_headsum(kk * kk, ones_bd)), 1e12)
        k_mod = k * (1.0 + (eta - 1.0) * ka_ref[...])
        bonus_ref[0, rows, :] = _headsum(r * k_mod * rk_ref[...], ones_bd) * v
        b_s = kk * eta

        cl = lw
        for sh in (1, 2, 4, 8, 16, 32):
            cl = cl + jnp.where(row >= sh, pltpu.roll(cl, sh, 0), 0.0)
        cl_end = cl[CHUNK - 1:CHUNK, :]
        e_ncl = jnp.exp(-cl)
        e_end = jnp.exp(cl_end - cl)
        at_ref[0, rows, :] = (-kk * jnp.exp(cl - lw)).astype(BF16)
        rt_ref[0, rows, :] = r * jnp.exp(cl)
        bt_ref[0, rows, :] = (b_s * e_ncl).astype(BF16)
        kt_ref[0, rows, :] = (k_mod * e_ncl).astype(BF16)
        vb_ref[0, rows, :] = v.astype(BF16)
        pc_ref[0, ci] = jnp.exp(cl_end)
        b_h = b_s * e_end
        k_h = k_mod * e_end
        for p in range(N_PAIRS):
            sl = slice(p * LANES, (p + 1) * LANES)
            lt_ref[0, ci, p] = jnp.concatenate([b_h[:, sl], k_h[:, sl]], axis=0).T.astype(BF16)


def _headsum(x, ones_bd):
    parts = []
    for gi in range(x.shape[1] // GROUP):
        xg = x[:, gi * GROUP:(gi + 1) * GROUP]
        hi = xg.astype(BF16)
        lo = (xg - hi.astype(F32)).astype(BF16)
        parts.append(_dot(hi, ones_bd) + _dot(lo, ones_bd))
    return jnp.concatenate(parts, axis=1)


def _head_ones():
    return jnp.kron(jnp.eye(GROUP // HEAD_DIM, dtype=F32),
                    jnp.ones((HEAD_DIM, HEAD_DIM), F32)).astype(BF16)


def _rwkv_feat_call(x, g, w, mu, w0, w2a2, a0, g2, k_k, k_a, r_k, tm=256):
    b, s, _ = x.shape
    n_chunks = tm // CHUNK
    const = lambda bb, i: (0, 0)
    vec = pl.BlockSpec((1, D_MODEL), const)
    tok = lambda bb, i: (bb, i, 0)
    tok_spec = pl.BlockSpec((1, tm, D_MODEL), tok)
    tok_shape = lambda dt: jax.ShapeDtypeStruct((b, s, D_MODEL), dt)
    return pl.pallas_call(
        functools.partial(_rwkv_feat_kernel, tm=tm),
        grid=(b, s // tm),
        in_specs=[tok_spec, vec,
                  pl.BlockSpec((D_MODEL, D_SHIFTED), const),
                  pl.BlockSpec((1, D_SHIFTED), const), vec,
                  pl.BlockSpec((LANES, 2 * D_MODEL), const), vec,
                  pl.BlockSpec((GATE_LORA, D_MODEL), const), vec, vec, vec,
                  pl.BlockSpec((GROUP, GROUP), const)],
        out_specs=[tok_spec, tok_spec, tok_spec, tok_spec,
                   pl.BlockSpec((1, n_chunks, N_PAIRS, LANES, LANES),
                                lambda bb, i: (bb, i, 0, 0, 0)),
                   tok_spec, tok_spec, tok_spec,
                   pl.BlockSpec((1, n_chunks, 1, D_MODEL), lambda bb, i: (bb, i, 0, 0))],
        out_shape=[tok_shape(BF16), tok_shape(BF16), tok_shape(BF16), tok_shape(BF16),
                   jax.ShapeDtypeStruct((b, s // CHUNK, N_PAIRS, LANES, LANES), BF16),
                   tok_shape(F32), tok_shape(F32), tok_shape(F32),
                   jax.ShapeDtypeStruct((b, s // CHUNK, 1, D_MODEL), F32)],
        scratch_shapes=[pltpu.VMEM((1, D_SHIFTED), F32)],
        compiler_params=pltpu.CompilerParams(dimension_semantics=("arbitrary", "arbitrary")),
        name="rwkv_feat",
    )(x, g, w, mu, w0, w2a2, a0, g2, k_k, k_a, r_k, _head_ones())


def _attn_kernel(q_ref, kc_ref, kp_ref, vc_ref, vp_ref, o_ref, m_ref, l_ref, *, tq):
    n = pl.program_id(2)
    blk = ATTN_BLOCK
    qi = lax.broadcasted_iota(jnp.int32, (blk, 1), 0)
    kc = lax.broadcasted_iota(jnp.int32, (1, blk), 1)
    upper = kc > qi
    diag_f = (kc == qi).astype(F32)
    upper_b = upper.astype(BF16)
    lower_b = (kc <= qi).astype(BF16)
    diag_b = diag_f.astype(BF16)
    lane = lax.broadcasted_iota(jnp.int32, (1, LANES), 1)
    head0 = lane < HEAD_DIM
    ones_v = jnp.ones((2 * blk, LANES), BF16)

    for i in range(tq // blk):
        r0 = i * blk
        m_tile = jnp.zeros((blk, LANES), F32)
        l_tile = jnp.ones((blk, LANES), F32)
        for hp in range(N_PAIRS):
            l0 = hp * LANES
            qs = q_ref[r0:r0 + blk, l0:l0 + LANES]
            if i == 0:
                kprev = kp_ref[:, l0:l0 + LANES]
                vprev = vp_ref[:, l0:l0 + LANES]
            else:
                kprev = kc_ref[r0 - blk:r0, l0:l0 + LANES]
                vprev = vc_ref[r0 - blk:r0, l0:l0 + LANES]
            k2 = jnp.concatenate([kprev, kc_ref[r0:r0 + blk, l0:l0 + LANES]], axis=0)
            v2 = jnp.concatenate([vprev, vc_ref[r0:r0 + blk, l0:l0 + LANES]], axis=0)
            zq = jnp.zeros_like(qs)
            q_st = jnp.concatenate([jnp.where(head0, qs, zq), jnp.where(head0, zq, qs)], axis=0)
            s = _dot_nt(q_st, k2)
            ps, ms = [], []
            for hh in range(2):
                s_prev = s[hh * blk:(hh + 1) * blk, :blk]
                s_cur = s[hh * blk:(hh + 1) * blk, blk:]
                if i == 0:
                    s_prev = jnp.where(n > 0, s_prev, -1e30)
                s_far = jnp.sum(s_prev * diag_f, axis=1, keepdims=True)
                s_tile = jnp.where(upper, s_prev, s_cur)
                m = jnp.maximum(jnp.max(s_tile, axis=1, keepdims=True), s_far)
                p = jnp.exp2(s_tile - m).astype(BF16)
                p_far = jnp.exp2(s_far - m).astype(BF16)
                ps.append(jnp.concatenate([p * upper_b + p_far * diag_b, p * lower_b], axis=1))
                ms.append(m)
            o2 = _dot(jnp.concatenate(ps, axis=0),
                      jnp.concatenate([v2, ones_v], axis=1))
            o = jnp.where(head0, o2[:blk, :LANES], o2[blk:, :LANES])
            o_ref[r0:r0 + blk, l0:l0 + LANES] = o.astype(BF16)
            for hh in range(2):
                mine = lane == 2 * hp + hh
                m_tile = jnp.where(mine, ms[hh], m_tile)
                l_tile = jnp.where(mine, o2[hh * blk:(hh + 1) * blk, LANES:], l_tile)
        m_ref[r0:r0 + blk, :] = m_tile
        l_ref[r0:r0 + blk, :] = l_tile


def _attn_call(q, k, v, tq=256):
    b, d, sub, _ = q.shape
    cur = lambda bb, r, n: (bb, r, n, 0)
    prev = lambda bb, r, n: (bb, r, jnp.maximum(n * (tq // ATTN_BLOCK) - 1, 0), 0)
    stat = jax.ShapeDtypeStruct((b, d, sub, LANES), F32)
    return pl.pallas_call(
        functools.partial(_attn_kernel, tq=tq),
        grid=(b, d, sub // tq),
        in_specs=[pl.BlockSpec((None, None, tq, D_MODEL), cur),
                  pl.BlockSpec((None, None, tq, D_MODEL), cur),
                  pl.BlockSpec((None, None, ATTN_BLOCK, D_MODEL), prev),
                  pl.BlockSpec((None, None, tq, D_MODEL), cur),
                  pl.BlockSpec((None, None, ATTN_BLOCK, D_MODEL), prev)],
        out_specs=[pl.BlockSpec((None, None, tq, D_MODEL), cur),
                   pl.BlockSpec((None, None, tq, LANES), cur),
                   pl.BlockSpec((None, None, tq, LANES), cur)],
        out_shape=[jax.ShapeDtypeStruct((b, d, sub, D_MODEL), BF16), stat, stat],
        compiler_params=pltpu.CompilerParams(
            dimension_semantics=("arbitrary", "arbitrary", "arbitrary")),
        name=f"dilated_attn_d{d}",
    )(q, k, k, v, v)


def _rwkv_kernel(at_ref, bt_ref, kt_ref, vb_ref, lt_ref, rt_ref, gate_ref, bonus_ref, pc_ref,
                 lnw_ref, lnb_ref, ones_ref, o_ref, h_ref):
    c_len = CHUNK

    @pl.when(pl.program_id(1) == 0)
    def _():
        h_ref[...] = jnp.zeros_like(h_ref)

    row = lax.broadcasted_iota(jnp.int32, (c_len, 1), 0)
    lane = lax.broadcasted_iota(jnp.int32, (1, LANES), 1)
    head0 = lane < HEAD_DIM

    col = lane % HEAD_DIM
    strict = row > col
    incl = row >= col
    eye = (row == col).astype(F32)
    n_rows = at_ref.shape[0]
    chains = [(bb, p) for bb in range(n_rows) for p in range(N_PAIRS)]
    cut = lambda ref: [ref[bb, :, p * LANES:(p + 1) * LANES] for bb, p in chains]
    at, rt, bt, kt, vb = (cut(ref) for ref in (at_ref, rt_ref, bt_ref, kt_ref, vb_ref))
    pcs = [pc_ref[bb, 0, :, p * LANES:(p + 1) * LANES] for bb, p in chains]
    top = lambda ts: [t[:c_len] for t in ts]
    bot = lambda ts: [t[c_len:] for t in ts]
    stack = lambda xs, ys: [jnp.concatenate([x, y], axis=0) for x, y in zip(xs, ys)]
    to_bf16 = lambda ts: [t.astype(BF16) for t in ts]

    def bd(y):
        zz = jnp.zeros_like(y)
        return jnp.concatenate([jnp.where(head0, y, zz), jnp.where(head0, zz, y)], axis=0)

    def pmm(xs, ys):
        return [_dot(x, bd(y)) for x, y in zip(xs, ys)]

    a_all = [_dot_nt(lhs, jnp.concatenate([bd(b), bd(kk_)], axis=0))
             for lhs, b, kk_ in zip(stack(at, to_bf16(rt)), bt, kt)]
    l_ab = [jnp.where(strict, a[:c_len, :LANES], 0.0) for a in a_all]
    a_ak = [jnp.where(strict, a[:c_len, LANES:], 0.0).astype(BF16) for a in a_all]
    a_rb = [jnp.where(incl, a[c_len:, :LANES], 0.0).astype(BF16) for a in a_all]
    a_rk = [jnp.where(incl, a[c_len:, LANES:], 0.0).astype(BF16) for a in a_all]

    l_b = to_bf16(l_ab)
    s_acc = [eye + l for l in l_ab]
    q_b = to_bf16(pmm(l_b, l_b))
    for _i in range(4):
        res = pmm(stack(to_bf16(s_acc), q_b), q_b)
        s_acc = [s + r_ for s, r_ in zip(s_acc, top(res))]
        q_b = to_bf16(bot(res))
    s_acc = [s + r_ for s, r_ in zip(s_acc, pmm(to_bf16(s_acc), q_b))]
    t_b = to_bf16(s_acc)

    res = pmm(stack(a_ak, a_rk), vb)
    x2_b, yv2 = to_bf16(top(res)), bot(res)
    a_hat_b = to_bf16(pmm(t_b, at))
    u_v_b = to_bf16(pmm(t_b, x2_b))
    r_hat = [r_ + d for r_, d in zip(rt, pmm(a_rb, a_hat_b))]
    y_v = [d + y2 for d, y2 in zip(pmm(a_rb, u_v_b), yv2)]

    lts = [lt_ref[bb, 0, p] for bb, p in chains]
    mns = [_dot(lt, jnp.concatenate(
        [jnp.concatenate([ah, jnp.zeros_like(ah)], axis=0),
         jnp.concatenate([uv, v_], axis=0)], axis=1))
        for lt, ah, uv, v_ in zip(lts, a_hat_b, u_v_b, vb)]
    m_p = [jnp.where(head0, mn[:c_len, :LANES], mn[c_len:, :LANES]) + eye * pc_
           for mn, pc_ in zip(mns, pcs)]
    n_p = [jnp.where(head0, mn[:c_len, LANES:], mn[c_len:, LANES:]) for mn in mns]

    res = pmm(stack(to_bf16(m_p), to_bf16(r_hat)), [h_ref[bb, p].astype(BF16) for bb, p in chains])
    for (bb, p), r_, n_ in zip(chains, res, n_p):
        h_ref[bb, p] = r_[:c_len] + n_
    ys = [r_[c_len:] + yv for r_, yv in zip(res, y_v)]

    ones_bd = ones_ref[...]
    for bb in range(n_rows):
        y = jnp.concatenate(ys[bb * N_PAIRS:(bb + 1) * N_PAIRS], axis=1)
        mean = _headsum(y, ones_bd) * (1.0 / HEAD_DIM)
        yc = y - mean
        var = _headsum(yc * yc, ones_bd) * (1.0 / HEAD_DIM)
        yn = yc * lax.rsqrt(var + GN_EPS) * lnw_ref[...] + lnb_ref[...]
        o_ref[bb] = ((yn + bonus_ref[bb]) * gate_ref[bb]).astype(BF16)


def _rwkv_call(feats, ln_w, ln_b):
    at, bt, kt, vb, lt, rt, gate, bonus, pc = feats
    b, s, _ = at.shape
    nb = 2 if b % 2 == 0 else 1
    const = lambda bb, t: (0, 0)
    vec = pl.BlockSpec((1, D_MODEL), const)
    tok = pl.BlockSpec((nb, CHUNK, D_MODEL), lambda bb, t: (bb, t, 0))
    return pl.pallas_call(
        _rwkv_kernel,
        grid=(b // nb, s // CHUNK),
        in_specs=[tok, tok, tok, tok,
                  pl.BlockSpec((nb, 1, N_PAIRS, LANES, LANES), lambda bb, t: (bb, t, 0, 0, 0)),
                  tok, tok, tok,
                  pl.BlockSpec((nb, 1, 1, D_MODEL), lambda bb, t: (bb, t, 0, 0)),
                  vec, vec, pl.BlockSpec((GROUP, GROUP), const)],
        out_specs=tok,
        out_shape=jax.ShapeDtypeStruct((b, s, D_MODEL), BF16),
        scratch_shapes=[pltpu.VMEM((nb, N_PAIRS, CHUNK, LANES), F32)],
        compiler_params=pltpu.CompilerParams(dimension_semantics=("arbitrary", "arbitrary")),
        name="rwkv7_mixer",
    )(at, bt, kt, vb, lt, rt, gate, bonus, pc, ln_w, ln_b, _head_ones())


def _merge_kernel(x_ref, g_ref, wg_ref, o1_ref, o4_ref, o16_ref, m1_ref, m4_ref, m16_ref,
                  l1_ref, l4_ref, l16_ref, ob_ref, pa_ref, pb_ref, wo_ref, e_ref, out_ref,
                  o_scr, stat_scr):
    x = x_ref[0]
    h = _rmsnorm(x, g_ref[...]).astype(BF16)
    gl = _dot(h, wg_ref[...])
    g_a = jax.nn.sigmoid(gl[:, :D_MODEL])
    g_b = jax.nn.sigmoid(gl[:, D_MODEL:])

    def natural_order(src_ref, dst_ref):
        _, d, rows, width = src_ref.shape
        for r in range(d):
            for cblk in range(width // LANES):
                val = src_ref[0, r, :, cblk * LANES:(cblk + 1) * LANES].astype(F32)
                if d == 1:
                    dst_ref[cblk] = val
                else:
                    dst_ref[cblk, pl.ds(r, rows, stride=d), :] = val

    ms, ls = [], []
    for gi, (m_ref, l_ref) in enumerate(((m1_ref, l1_ref), (m4_ref, l4_ref), (m16_ref, l16_ref))):
        natural_order(m_ref, stat_scr.at[2 * gi])
        natural_order(l_ref, stat_scr.at[2 * gi + 1])
        ms.append(stat_scr[2 * gi, 0])
        ls.append(stat_scr[2 * gi + 1, 0])
    mx = jnp.maximum(jnp.maximum(ms[0], ms[1]), ms[2])
    es = [jnp.exp2(m - mx) for m in ms]
    den = es[0] * ls[0] + es[1] * ls[1] + es[2] * ls[2]
    expand = e_ref[...]
    o_a = jnp.zeros(x.shape, F32)
    for e, o_ref in zip(es, (o1_ref, o4_ref, o16_ref)):
        w = e / den
        hi = w.astype(BF16)
        lo = (w - hi.astype(F32)).astype(BF16)
        natural_order(o_ref, o_scr)
        o_g = jnp.concatenate([o_scr[p] for p in range(N_PAIRS)], axis=1)
        o_a = o_a + (_dot(hi, expand) + _dot(lo, expand)) * o_g

    merged = (g_a * _dot(o_a.astype(BF16), pa_ref[...])
              + g_b * _dot(ob_ref[0], pb_ref[...]))
    out_ref[0] = x + _dot(merged.astype(BF16), wo_ref[...])


def _merge_call(x, g, wg, os_, ms, ls, ob, pa, pb, wo, tm=256):
    b, s, _ = x.shape
    expand = jnp.kron(jnp.eye(N_HEADS, dtype=F32), jnp.ones((1, HEAD_DIM), F32))
    expand = jnp.concatenate([expand, jnp.zeros((LANES - N_HEADS, D_MODEL), F32)], 0).astype(BF16)
    const = lambda bb, i: (0, 0)
    wide = pl.BlockSpec((1, tm, D_MODEL), lambda bb, i: (bb, i, 0))
    sq = pl.BlockSpec((D_MODEL, D_MODEL), const)
    res = lambda a: pl.BlockSpec((1, a.shape[1], tm // a.shape[1], a.shape[3]),
                                 lambda bb, i: (bb, 0, i, 0))
    return pl.pallas_call(
        _merge_kernel,
        grid=(b, s // tm),
        in_specs=[wide, pl.BlockSpec((1, D_MODEL), const),
                  pl.BlockSpec((D_MODEL, 2 * D_MODEL), const),
                  *[res(a) for a in (*os_, *ms, *ls)], wide, sq, sq, sq,
                  pl.BlockSpec((LANES, D_MODEL), const)],
        out_specs=wide,
        out_shape=jax.ShapeDtypeStruct((b, s, D_MODEL), F32),
        scratch_shapes=[pltpu.VMEM((N_PAIRS, tm, LANES), F32),
                        pltpu.VMEM((len(ms) + len(ls), 1, tm, LANES), F32)],
        compiler_params=pltpu.CompilerParams(dimension_semantics=("arbitrary", "arbitrary")),
        name="merge_proj",
    )(x, g, wg, *os_, *ms, *ls, ob, pa, pb, wo, expand)


def _ffn_kernel(x_ref, g_ref, wg_ref, wu_ref, wd_ref, gf_ref, out_ref, *, final_norm):
    x = x_ref[...]
    h = _rmsnorm(x, g_ref[...]).astype(BF16)
    act = (jax.nn.silu(_dot(h, wg_ref[...])) * _dot(h, wu_ref[...])).astype(BF16)
    x2 = x + _dot(act, wd_ref[...])
    out_ref[...] = _rmsnorm(x2, gf_ref[...]) if final_norm else x2


def _ffn_call(x2, g, wg, wu, wd, gf, final_norm, tm=512):
    t = x2.shape[0]
    row = lambda i: (i, 0)
    const = lambda i: (0, 0)
    vec = pl.BlockSpec((1, D_MODEL), const)
    resident = lambda shape: pl.BlockSpec(shape, const, pipeline_mode=pl.Buffered(1))
    return pl.pallas_call(
        functools.partial(_ffn_kernel, final_norm=final_norm),
        grid=(t // tm,),
        in_specs=[pl.BlockSpec((tm, D_MODEL), row), vec,
                  resident((D_MODEL, D_FF)), resident((D_MODEL, D_FF)),
                  resident((D_FF, D_MODEL)), vec],
        out_specs=pl.BlockSpec((tm, D_MODEL), row),
        out_shape=jax.ShapeDtypeStruct((t, D_MODEL), F32),
        compiler_params=pltpu.CompilerParams(dimension_semantics=("arbitrary",)),
        name="ffn_final",
    )(x2, g, wg, wu, wd, gf)


def _rotary_tables(seq):
    half = ROPE_DIM // 2
    inv_freq = ROPE_THETA ** (-jnp.arange(half, dtype=F32) * (2.0 / ROPE_DIM))
    ang = jnp.arange(seq, dtype=jnp.int32).astype(F32)[:, None] * inv_freq[None, :]
    cos, sin = jnp.cos(ang), jnp.sin(ang)
    zeros = lambda n: jnp.zeros((seq, n), F32)
    rest = HEAD_DIM - ROPE_DIM
    per_head = lambda parts: jnp.tile(jnp.concatenate(parts, axis=1), (1, LANES // HEAD_DIM))
    c = per_head([cos, cos, jnp.ones((seq, rest), F32)])
    s1 = per_head([zeros(half), sin, zeros(rest)])
    s2 = per_head([-sin, zeros(half), zeros(rest)])
    return c, s1, s2


def kernel(x, norm_mix_g, w_in, shift_mu, decay_w0, decay_w2, iclr_a0, iclr_a2, gate_g2, k_k, k_a, r_k, ln_x_w, ln_x_b, proj_attn, proj_rwkv, w_out, norm_ffn_g, ffn_w_gate, ffn_w_up, ffn_w_down, norm_final_g):
    b, s, d = x.shape
    assert d == D_MODEL and s % (16 * 256) == 0
    depth = w_in.shape[0]
    c, s1, s2 = _rotary_tables(s)
    vec = lambda a: a.reshape(1, -1).astype(F32)
    n_attn = 3 * D_MODEL
    for l in range(depth):
        g_mix = vec(norm_mix_g[l])
        w = w_in[l]
        qkv = _qkv_call(x, g_mix, w[:, :n_attn].astype(BF16), c, s1, s2)
        zero = jnp.zeros((DECAY_LORA, D_MODEL), F32)
        w2a2 = jnp.concatenate(
            [jnp.concatenate([decay_w2[l], zero], axis=1),
             jnp.concatenate([zero, iclr_a2[l]], axis=1)], axis=0).astype(BF16)
        feats = _rwkv_feat_call(x, g_mix, w[:, n_attn:n_attn + D_SHIFTED].astype(BF16),
                                vec(shift_mu[l]), vec(decay_w0[l]), w2a2, vec(iclr_a0[l]),
                                gate_g2[l].astype(BF16), vec(k_k[l]), vec(k_a[l]), vec(r_k[l]))
        os_, ms, ls = [], [], []
        for gi in range(len(DILATED_GROUPS)):
            o_g, m_g, l_g = _attn_call(*qkv[3 * gi:3 * gi + 3])
            os_.append(o_g)
            ms.append(m_g)
            ls.append(l_g)
        o_b = _rwkv_call(feats, vec(ln_x_w[l]), vec(ln_x_b[l]))
        x = _merge_call(x, g_mix, w[:, n_attn + D_SHIFTED:].astype(BF16), os_, ms, ls, o_b,
                        proj_attn[l].astype(BF16), proj_rwkv[l].astype(BF16),
                        w_out[l].astype(BF16))
        x = _ffn_call(x.reshape(b * s, d), vec(norm_ffn_g[l]), ffn_w_gate[l].astype(BF16),
                      ffn_w_up[l].astype(BF16), ffn_w_down[l].astype(BF16),
                      vec(norm_final_g), final_norm=(l == depth - 1)).reshape(b, s, d)
    return x
```

```python
import functools

import jax
import jax.numpy as jnp
from jax import lax
from jax.experimental import pallas as pl
from jax.experimental.pallas import tpu as pltpu

F32 = jnp.float32
BF16 = jnp.bfloat16

D_MODEL = 1024
HEAD_DIM = 64
N_HEADS = 16
ROPE_DIM = 16
ROPE_THETA = 500000.0
DILATED_GROUPS = ((128, 1), (512, 4), (2048, 16))
ATTN_BLOCK = 128
DECAY_LORA = 64
ICLR_LORA = 64
GATE_LORA = 128
D_FF = 2816
RMS_EPS = 1e-6
GN_EPS = 64e-5
D_SHIFTED = 3 * D_MODEL + DECAY_LORA + ICLR_LORA + GATE_LORA

LANES = 128
SUBLANES = 8
N_PAIRS = D_MODEL // LANES
CHUNK = 64
GROUP = 256

LOG2_E = 1.4426950408889634
NT_DIMS = (((1,), (1,)), ((), ()))


def _dot(a, b):
    return jnp.dot(a, b, preferred_element_type=F32)


def _dot_nt(a, b):
    return lax.dot_general(a, b, NT_DIMS, preferred_element_type=F32)


def _rmsnorm(x, g):
    return x * lax.rsqrt(jnp.mean(x * x, axis=-1, keepdims=True) + RMS_EPS) * g


def _qkv_kernel(x_ref, g_ref, w_ref, c_ref, s1_ref, s2_ref, *refs, tm):
    outs, stages = refs[:-2], refs[-2:]
    dilations = [d for _, d in DILATED_GROUPS]
    assert dilations[0] == 1
    h = _rmsnorm(x_ref[0], g_ref[...]).astype(BF16)
    p = _dot(h, w_ref[...])
    c, s1, s2 = c_ref[...], s1_ref[...], s2_ref[...]

    def rot(t):
        return t * c + pltpu.roll(t, 8, 1) * s1 + pltpu.roll(t, LANES - 8, 1) * s2

    for blk in range(N_PAIRS):
        lanes = slice(blk * LANES, (blk + 1) * LANES)
        vals = (rot(p[:, lanes]) * (HEAD_DIM ** -0.5 * LOG2_E),
                rot(p[:, D_MODEL + blk * LANES:D_MODEL + (blk + 1) * LANES]),
                p[:, 2 * D_MODEL + blk * LANES:2 * D_MODEL + (blk + 1) * LANES])
        for j, val in enumerate(vals):
            stages[0][j * N_PAIRS + blk] = val
            outs[j][0, 0, :, lanes] = val.astype(BF16)

    d_prev = 1
    for gi in range(1, len(dilations)):
        d = dilations[gi]
        step = d // d_prev
        assert step * d_prev == d
        rows_prev, rows = tm // d_prev, tm // d
        src_ref, dst_ref = stages[(gi - 1) % 2], stages[gi % 2]
        for j in range(3):
            for blk in range(N_PAIRS):
                slab = j * N_PAIRS + blk
                lanes = slice(blk * LANES, (blk + 1) * LANES)
                for r_prev in range(d_prev):
                    for off in range(step):
                        r = d_prev * off + r_prev
                        val = src_ref[slab, pl.ds(r_prev * rows_prev + off, rows, stride=step), :]
                        outs[3 * gi + j][0, r, :, lanes] = val.astype(BF16)
                        if gi + 1 < len(dilations):
                            dst_ref[slab, r * rows:(r + 1) * rows, :] = val
        d_prev = d


def _qkv_call(x, g, w, c, s1, s2, tm=256):
    b, s, _ = x.shape
    const = lambda bb, i: (0, 0)
    pos = lambda bb, i: (i, 0)
    out_specs, out_shape = [], []
    for _, d in DILATED_GROUPS:
        for _j in range(3):
            out_specs.append(pl.BlockSpec((1, d, tm // d, D_MODEL), lambda bb, i: (bb, 0, i, 0)))
            out_shape.append(jax.ShapeDtypeStruct((b, d, s // d, D_MODEL), BF16))
    return pl.pallas_call(
        functools.partial(_qkv_kernel, tm=tm),
        grid=(b, s // tm),
        in_specs=[pl.BlockSpec((1, tm, D_MODEL), lambda bb, i: (bb, i, 0)),
                  pl.BlockSpec((1, D_MODEL), const),
                  pl.BlockSpec((D_MODEL, 3 * D_MODEL), const),
                  pl.BlockSpec((tm, LANES), pos), pl.BlockSpec((tm, LANES), pos),
                  pl.BlockSpec((tm, LANES), pos)],
        out_specs=out_specs,
        out_shape=out_shape,
        scratch_shapes=[pltpu.VMEM((3 * N_PAIRS, tm, LANES), F32),
                        pltpu.VMEM((3 * N_PAIRS, tm, LANES), F32)],
        compiler_params=pltpu.CompilerParams(dimension_semantics=("arbitrary", "arbitrary")),
        name="qkv_proj",
    )(x, g, w, c, s1, s2)


def _rwkv_feat_kernel(x_ref, g_ref, w_ref, mu_ref, w0_ref, w2a2_ref, a0_ref, g2_ref, kk_ref,
                      ka_ref, rk_ref, ones_ref,
                      at_ref, bt_ref, kt_ref, vb_ref, lt_ref, rt_ref, gate_ref, bonus_ref, pc_ref,
                      carry_ref, *, tm):
    @pl.when(pl.program_id(1) == 0)
    def _():
        carry_ref[...] = jnp.zeros_like(carry_ref)

    lane = lax.broadcasted_iota(jnp.int32, (1, LANES), 1)
    head0 = lane < HEAD_DIM
    ones_bd = ones_ref[...]

    h = _rmsnorm(x_ref[0], g_ref[...]).astype(BF16)
    cols = _dot(h, w_ref[...])
    row8 = lax.broadcasted_iota(jnp.int32, (SUBLANES, 1), 0)
    prev = pltpu.roll(cols, 1, 0)
    prev = jnp.concatenate([jnp.where(row8 == 0, carry_ref[...], prev[:SUBLANES]),
                            prev[SUBLANES:]], axis=0)
    carry_ref[...] = cols[tm - 1:tm, :]
    mu = mu_ref[...]
    xm_all = cols * (1.0 - mu) + prev * mu

    def prefix_sum(x):
        outs, total = [], None
        for gi in range(x.shape[0] // SUBLANES):
            xg = x[gi * SUBLANES:(gi + 1) * SUBLANES]
            for sh in (1, 2, 4):
                xg = xg + jnp.where(row8 >= sh, pltpu.roll(xg, sh, 0), 0.0)
            if total is not None:
                xg = xg + total
            total = xg[SUBLANES - 1:SUBLANES]
            outs.append(xg)
        return jnp.concatenate(outs, axis=0)

    for ci in range(tm // CHUNK):
        rows = slice(ci * CHUNK, (ci + 1) * CHUNK)
        xm = xm_all[rows]
        r = xm[:, 0:D_MODEL]
        k = xm[:, D_MODEL:2 * D_MODEL]
        v = xm[:, 2 * D_MODEL:3 * D_MODEL]
        slab = xm[:, 3 * D_MODEL:3 * D_MODEL + LANES]
        g_lo = xm[:, 3 * D_MODEL + LANES:]
        z = jnp.where(head0, jnp.tanh(slab), slab).astype(BF16)
        lora = _dot(z, w2a2_ref[...])
        lw = jax.nn.sigmoid(w0_ref[...] + lora[:, :D_MODEL]) * (-(jnp.e ** -0.5))
        eta = jax.nn.sigmoid(a0_ref[...] + lora[:, D_MODEL:])
        gate_ref[0, rows, :] = _dot(jax.nn.sigmoid(g_lo).astype(BF16), g2_ref[...])
        kk = k * kk_ref[...]
        kk = kk * jnp.minimum(lax.rsqrt(_headsum(kk * kk, ones_bd)), 1e12)
        k_mod = k * (1.0 + (eta - 1.0) * ka_ref[...])
        bonus_ref[0, rows, :] = _headsum(r * k_mod * rk_ref[...], ones_bd) * v
        b_s = kk * eta

        cl = prefix_sum(lw)
        cl_end = cl[CHUNK - 1:CHUNK, :]
        e_ncl = jnp.exp(-cl)
        e_end = jnp.exp(cl_end - cl)
        at_ref[0, rows, :] = (-kk * jnp.exp(cl - lw)).astype(BF16)
        rt_ref[0, rows, :] = r * jnp.exp(cl)
        bt_ref[0, rows, :] = (b_s * e_ncl).astype(BF16)
        kt_ref[0, rows, :] = (k_mod * e_ncl).astype(BF16)
        vb_ref[0, rows, :] = v.astype(BF16)
        pc_ref[0, ci] = jnp.exp(cl_end)
        b_h = b_s * e_end
        k_h = k_mod * e_end
        for p in range(N_PAIRS):
            sl = slice(p * LANES, (p + 1) * LANES)
            lt_ref[0, ci, p] = jnp.concatenate([b_h[:, sl], k_h[:, sl]], axis=0).T.astype(BF16)


def _headsum(x, ones_bd):
    parts = []
    for gi in range(x.shape[1] // GROUP):
        xg = x[:, gi * GROUP:(gi + 1) * GROUP]
        hi = xg.astype(BF16)
        lo = (xg - hi.astype(F32)).astype(BF16)
        parts.append(_dot(hi, ones_bd) + _dot(lo, ones_bd))
    return jnp.concatenate(parts, axis=1)


def _head_ones():
    return jnp.kron(jnp.eye(GROUP // HEAD_DIM, dtype=F32),
                    jnp.ones((HEAD_DIM, HEAD_DIM), F32)).astype(BF16)


def _rwkv_feat_call(x, g, w, mu, w0, w2a2, a0, g2, k_k, k_a, r_k, tm=256):
    b, s, _ = x.shape
    n_chunks = tm // CHUNK
    const = lambda bb, i: (0, 0)
    vec = pl.BlockSpec((1, D_MODEL), const)
    tok = lambda bb, i: (bb, i, 0)
    tok_spec = pl.BlockSpec((1, tm, D_MODEL), tok)
    tok_shape = lambda dt: jax.ShapeDtypeStruct((b, s, D_MODEL), dt)
    return pl.pallas_call(
        functools.partial(_rwkv_feat_kernel, tm=tm),
        grid=(b, s // tm),
        in_specs=[tok_spec, vec,
                  pl.BlockSpec((D_MODEL, D_SHIFTED), const),
                  pl.BlockSpec((1, D_SHIFTED), const), vec,
                  pl.BlockSpec((LANES, 2 * D_MODEL), const), vec,
                  pl.BlockSpec((GATE_LORA, D_MODEL), const), vec, vec, vec,
                  pl.BlockSpec((GROUP, GROUP), const)],
        out_specs=[tok_spec, tok_spec, tok_spec, tok_spec,
                   pl.BlockSpec((1, n_chunks, N_PAIRS, LANES, LANES),
                                lambda bb, i: (bb, i, 0, 0, 0)),
                   tok_spec, tok_spec, tok_spec,
                   pl.BlockSpec((1, n_chunks, 1, D_MODEL), lambda bb, i: (bb, i, 0, 0))],
        out_shape=[tok_shape(BF16), tok_shape(BF16), tok_shape(BF16), tok_shape(BF16),
                   jax.ShapeDtypeStruct((b, s // CHUNK, N_PAIRS, LANES, LANES), BF16),
                   tok_shape(F32), tok_shape(F32), tok_shape(F32),
                   jax.ShapeDtypeStruct((b, s // CHUNK, 1, D_MODEL), F32)],
        scratch_shapes=[pltpu.VMEM((1, D_SHIFTED), F32)],
        compiler_params=pltpu.CompilerParams(dimension_semantics=("arbitrary", "arbitrary")),
        name="rwkv_feat",
    )(x, g, w, mu, w0, w2a2, a0, g2, k_k, k_a, r_k, _head_ones())


def _attn_kernel(q_ref, kc_ref, kp_ref, vc_ref, vp_ref, o_ref, m_ref, l_ref, *, tq):
    n = pl.program_id(2)
    blk = ATTN_BLOCK
    qi = lax.broadcasted_iota(jnp.int32, (blk, 1), 0)
    kc = lax.broadcasted_iota(jnp.int32, (1, blk), 1)
    upper = kc > qi
    diag_f = (kc == qi).astype(F32)
    upper_b = upper.astype(BF16)
    lower_b = (kc <= qi).astype(BF16)
    diag_b = diag_f.astype(BF16)
    lane = lax.broadcasted_iota(jnp.int32, (1, LANES), 1)
    head0 = lane < HEAD_DIM
    ones_v = jnp.ones((2 * blk, LANES), BF16)

    for i in range(tq // blk):
        r0 = i * blk
        m_tile = jnp.zeros((blk, LANES), F32)
        l_tile = jnp.ones((blk, LANES), F32)
        for hp in range(N_PAIRS):
            l0 = hp * LANES
            qs = q_ref[r0:r0 + blk, l0:l0 + LANES]
            if i == 0:
                kprev = kp_ref[:, l0:l0 + LANES]
                vprev = vp_ref[:, l0:l0 + LANES]
            else:
                kprev = kc_ref[r0 - blk:r0, l0:l0 + LANES]
                vprev = vc_ref[r0 - blk:r0, l0:l0 + LANES]
            k2 = jnp.concatenate([kprev, kc_ref[r0:r0 + blk, l0:l0 + LANES]], axis=0)
            v2 = jnp.concatenate([vprev, vc_ref[r0:r0 + blk, l0:l0 + LANES]], axis=0)
            zq = jnp.zeros_like(qs)
            q_st = jnp.concatenate([jnp.where(head0, qs, zq), jnp.where(head0, zq, qs)], axis=0)
            s = _dot_nt(q_st, k2)
            ps, ms = [], []
            for hh in range(2):
                s_prev = s[hh * blk:(hh + 1) * blk, :blk]
                s_cur = s[hh * blk:(hh + 1) * blk, blk:]
                if i == 0:
                    s_prev = jnp.where(n > 0, s_prev, -1e30)
                s_far = jnp.sum(s_prev * diag_f, axis=1, keepdims=True)
                s_tile = jnp.where(upper, s_prev, s_cur)
                m = jnp.maximum(jnp.max(s_tile, axis=1, keepdims=True), s_far)
                p = jnp.exp2(s_tile - m).astype(BF16)
                p_far = jnp.exp2(s_far - m).astype(BF16)
                ps.append(jnp.concatenate([p * upper_b + p_far * diag_b, p * lower_b], axis=1))
                ms.append(m)
            o2 = _dot(jnp.concatenate(ps, axis=0),
                      jnp.concatenate([v2, ones_v], axis=1))
            o = jnp.where(head0, o2[:blk, :LANES], o2[blk:, :LANES])
            o_ref[r0:r0 + blk, l0:l0 + LANES] = o.astype(BF16)
            for hh in range(2):
                mine = lane == 2 * hp + hh
                m_tile = jnp.where(mine, ms[hh], m_tile)
                l_tile = jnp.where(mine, o2[hh * blk:(hh + 1) * blk, LANES:], l_tile)
        m_ref[r0:r0 + blk, :] = m_tile
        l_ref[r0:r0 + blk, :] = l_tile


def _attn_call(q, k, v, tq=256):
    b, d, sub, _ = q.shape
    cur = lambda bb, r, n: (bb, r, n, 0)
    prev = lambda bb, r, n: (bb, r, jnp.maximum(n * (tq // ATTN_BLOCK) - 1, 0), 0)
    stat = jax.ShapeDtypeStruct((b, d, sub, LANES), F32)
    return pl.pallas_call(
        functools.partial(_attn_kernel, tq=tq),
        grid=(b, d, sub // tq),
        in_specs=[pl.BlockSpec((None, None, tq, D_MODEL), cur),
                  pl.BlockSpec((None, None, tq, D_MODEL), cur),
                  pl.BlockSpec((None, None, ATTN_BLOCK, D_MODEL), prev),
                  pl.BlockSpec((None, None, tq, D_MODEL), cur),
                  pl.BlockSpec((None, None, ATTN_BLOCK, D_MODEL), prev)],
        out_specs=[pl.BlockSpec((None, None, tq, D_MODEL), cur),
                   pl.BlockSpec((None, None, tq, LANES), cur),
                   pl.BlockSpec((None, None, tq, LANES), cur)],
        out_shape=[jax.ShapeDtypeStruct((b, d, sub, D_MODEL), BF16), stat, stat],
        compiler_params=pltpu.CompilerParams(
            dimension_semantics=("arbitrary", "arbitrary", "arbitrary")),
        name=f"dilated_attn_d{d}",
    )(q, k, k, v, v)


def _rwkv_kernel(at_ref, bt_ref, kt_ref, vb_ref, lt_ref, rt_ref, gate_ref, bonus_ref, pc_ref,
                 lnw_ref, lnb_ref, ones_ref, o_ref, h_ref):
    c_len = CHUNK

    @pl.when(pl.program_id(1) == 0)
    def _():
        h_ref[...] = jnp.zeros_like(h_ref)

    row = lax.broadcasted_iota(jnp.int32, (c_len, 1), 0)
    lane = lax.broadcasted_iota(jnp.int32, (1, LANES), 1)
    head0 = lane < HEAD_DIM

    col = lane % HEAD_DIM
    strict = row > col
    incl = row >= col
    eye = (row == col).astype(F32)
    n_rows = at_ref.shape[0]
    n_chunks = at_ref.shape[1] // c_len
    chains = [(ci, bb, p) for ci in range(n_chunks) for bb in range(n_rows)
              for p in range(N_PAIRS)]
    cut = lambda ref: [ref[bb, ci * c_len:(ci + 1) * c_len, p * LANES:(p + 1) * LANES]
                       for ci, bb, p in chains]
    at, rt, bt, kt, vb = (cut(ref) for ref in (at_ref, rt_ref, bt_ref, kt_ref, vb_ref))
    pcs = [pc_ref[bb, ci, :, p * LANES:(p + 1) * LANES] for ci, bb, p in chains]
    top = lambda ts: [t[:c_len] for t in ts]
    bot = lambda ts: [t[c_len:] for t in ts]
    stack = lambda xs, ys: [jnp.concatenate([x, y], axis=0) for x, y in zip(xs, ys)]
    to_bf16 = lambda ts: [t.astype(BF16) for t in ts]

    def bd(y):
        zz = jnp.zeros_like(y)
        return jnp.concatenate([jnp.where(head0, y, zz), jnp.where(head0, zz, y)], axis=0)

    def pmm(xs, ys):
        return [_dot(x, bd(y)) for x, y in zip(xs, ys)]

    a_all = [_dot_nt(lhs, jnp.concatenate([bd(b), bd(kk_)], axis=0))
             for lhs, b, kk_ in zip(stack(at, to_bf16(rt)), bt, kt)]
    l_ab = [jnp.where(strict, a[:c_len, :LANES], 0.0) for a in a_all]
    a_ak = [jnp.where(strict, a[:c_len, LANES:], 0.0).astype(BF16) for a in a_all]
    a_rb = [jnp.where(incl, a[c_len:, :LANES], 0.0).astype(BF16) for a in a_all]
    a_rk = [jnp.where(incl, a[c_len:, LANES:], 0.0).astype(BF16) for a in a_all]

    l_b = to_bf16(l_ab)
    s_acc = [eye + l for l in l_ab]
    q_b = to_bf16(pmm(l_b, l_b))
    for _i in range(4):
        res = pmm(stack(to_bf16(s_acc), q_b), q_b)
        s_acc = [s + r_ for s, r_ in zip(s_acc, top(res))]
        q_b = to_bf16(bot(res))
    s_acc = [s + r_ for s, r_ in zip(s_acc, pmm(to_bf16(s_acc), q_b))]
    t_b = to_bf16(s_acc)

    res = pmm(stack(a_ak, a_rk), vb)
    x2_b, yv2 = to_bf16(top(res)), bot(res)
    def pmm2(xs, ys, zs):
        return [_dot(x, jnp.concatenate([bd(y), bd(z)], axis=1)) for x, y, z in zip(xs, ys, zs)]

    res = pmm2(t_b, at, x2_b)
    a_hat_b = to_bf16([r_[:, :LANES] for r_ in res])
    u_v_b = to_bf16([r_[:, LANES:] for r_ in res])
    res = pmm2(a_rb, a_hat_b, u_v_b)
    r_hat = [r_ + d[:, :LANES] for r_, d in zip(rt, res)]
    y_v = [d[:, LANES:] + y2 for d, y2 in zip(res, yv2)]

    lts = [lt_ref[bb, ci, p] for ci, bb, p in chains]
    mns = [_dot(lt, jnp.concatenate(
        [jnp.concatenate([ah, jnp.zeros_like(ah)], axis=0),
         jnp.concatenate([uv, v_], axis=0)], axis=1))
        for lt, ah, uv, v_ in zip(lts, a_hat_b, u_v_b, vb)]
    m_p = [jnp.where(head0, mn[:c_len, :LANES], mn[c_len:, :LANES]) + eye * pc_
           for mn, pc_ in zip(mns, pcs)]
    n_p = [jnp.where(head0, mn[:c_len, LANES:], mn[c_len:, LANES:]) for mn in mns]

    lhs = stack(to_bf16(m_p), to_bf16(r_hat))
    states = [h_ref[bb, p] for bb in range(n_rows) for p in range(N_PAIRS)]
    per_chunk = n_rows * N_PAIRS
    ones_bd = ones_ref[...]
    for ci in range(n_chunks):
        sel = slice(ci * per_chunk, (ci + 1) * per_chunk)
        res = pmm(lhs[sel], to_bf16(states))
        states = [r_[:c_len] + n_ for r_, n_ in zip(res, n_p[sel])]
        ys = [r_[c_len:] + yv for r_, yv in zip(res, y_v[sel])]
        rows = slice(ci * c_len, (ci + 1) * c_len)
        for bb in range(n_rows):
            y = jnp.concatenate(ys[bb * N_PAIRS:(bb + 1) * N_PAIRS], axis=1)
            mean = _headsum(y, ones_bd) * (1.0 / HEAD_DIM)
            yc = y - mean
            var = _headsum(yc * yc, ones_bd) * (1.0 / HEAD_DIM)
            yn = yc * lax.rsqrt(var + GN_EPS) * lnw_ref[...] + lnb_ref[...]
            o_ref[bb, rows, :] = ((yn + bonus_ref[bb, rows, :]) * gate_ref[bb, rows, :]).astype(BF16)
    for j, h_new in enumerate(states):
        h_ref[j // N_PAIRS, j % N_PAIRS] = h_new


def _rwkv_call(feats, ln_w, ln_b, n_chunks=2):
    at, bt, kt, vb, lt, rt, gate, bonus, pc = feats
    b, s, _ = at.shape
    nb = 2 if b % 2 == 0 else 1
    span = n_chunks * CHUNK
    const = lambda bb, t: (0, 0)
    vec = pl.BlockSpec((1, D_MODEL), const)
    tok = pl.BlockSpec((nb, span, D_MODEL), lambda bb, t: (bb, t, 0))
    return pl.pallas_call(
        _rwkv_kernel,
        grid=(b // nb, s // span),
        in_specs=[tok, tok, tok, tok,
                  pl.BlockSpec((nb, n_chunks, N_PAIRS, LANES, LANES),
                               lambda bb, t: (bb, t, 0, 0, 0)),
                  tok, tok, tok,
                  pl.BlockSpec((nb, n_chunks, 1, D_MODEL), lambda bb, t: (bb, t, 0, 0)),
                  vec, vec, pl.BlockSpec((GROUP, GROUP), const)],
        out_specs=tok,
        out_shape=jax.ShapeDtypeStruct((b, s, D_MODEL), BF16),
        scratch_shapes=[pltpu.VMEM((nb, N_PAIRS, CHUNK, LANES), F32)],
        compiler_params=pltpu.CompilerParams(dimension_semantics=("arbitrary", "arbitrary")),
        name="rwkv7_mixer",
    )(at, bt, kt, vb, lt, rt, gate, bonus, pc, ln_w, ln_b, _head_ones())


def _merge_kernel(x_ref, g_ref, wg_ref, o1_ref, o4_ref, o16_ref, m1_ref, m4_ref, m16_ref,
                  l1_ref, l4_ref, l16_ref, ob_ref, pa_ref, pb_ref, wo_ref, e_ref, out_ref,
                  o_scr, stat_scr):
    x = x_ref[0]
    h = _rmsnorm(x, g_ref[...]).astype(BF16)
    gl = _dot(h, wg_ref[...])
    g_a = jax.nn.sigmoid(gl[:, :D_MODEL])
    g_b = jax.nn.sigmoid(gl[:, D_MODEL:])

    def natural_order(src_ref, dst_ref):
        _, d, rows, width = src_ref.shape
        for r in range(d):
            for cblk in range(width // LANES):
                val = src_ref[0, r, :, cblk * LANES:(cblk + 1) * LANES].astype(F32)
                if d == 1:
                    dst_ref[cblk] = val
                else:
                    dst_ref[cblk, pl.ds(r, rows, stride=d), :] = val

    ms, ls = [], []
    for gi, (m_ref, l_ref) in enumerate(((m1_ref, l1_ref), (m4_ref, l4_ref), (m16_ref, l16_ref))):
        natural_order(m_ref, stat_scr.at[2 * gi])
        natural_order(l_ref, stat_scr.at[2 * gi + 1])
        ms.append(stat_scr[2 * gi, 0])
        ls.append(stat_scr[2 * gi + 1, 0])
    mx = jnp.maximum(jnp.maximum(ms[0], ms[1]), ms[2])
    es = [jnp.exp2(m - mx) for m in ms]
    den = es[0] * ls[0] + es[1] * ls[1] + es[2] * ls[2]
    expand = e_ref[...]
    o_a = jnp.zeros(x.shape, F32)
    for e, o_ref in zip(es, (o1_ref, o4_ref, o16_ref)):
        w = e / den
        hi = w.astype(BF16)
        lo = (w - hi.astype(F32)).astype(BF16)
        natural_order(o_ref, o_scr)
        o_g = jnp.concatenate([o_scr[p] for p in range(N_PAIRS)], axis=1)
        o_a = o_a + _dot(jnp.concatenate([hi, lo], axis=1), expand) * o_g

    merged = (g_a * _dot(o_a.astype(BF16), pa_ref[...])
              + g_b * _dot(ob_ref[0], pb_ref[...]))
    out_ref[0] = x + _dot(merged.astype(BF16), wo_ref[...])


def _merge_call(x, g, wg, os_, ms, ls, ob, pa, pb, wo, tm=256):
    b, s, _ = x.shape
    expand = jnp.kron(jnp.eye(N_HEADS, dtype=F32), jnp.ones((1, HEAD_DIM), F32))
    expand = jnp.concatenate([expand, jnp.zeros((LANES - N_HEADS, D_MODEL), F32)], 0).astype(BF16)
    expand = jnp.concatenate([expand, expand], axis=0)
    const = lambda bb, i: (0, 0)
    wide = pl.BlockSpec((1, tm, D_MODEL), lambda bb, i: (bb, i, 0))
    sq = pl.BlockSpec((D_MODEL, D_MODEL), const)
    res = lambda a: pl.BlockSpec((1, a.shape[1], tm // a.shape[1], a.shape[3]),
                                 lambda bb, i: (bb, 0, i, 0))
    return pl.pallas_call(
        _merge_kernel,
        grid=(b, s // tm),
        in_specs=[wide, pl.BlockSpec((1, D_MODEL), const),
                  pl.BlockSpec((D_MODEL, 2 * D_MODEL), const),
                  *[res(a) for a in (*os_, *ms, *ls)], wide, sq, sq, sq,
                  pl.BlockSpec((2 * LANES, D_MODEL), const)],
        out_specs=wide,
        out_shape=jax.ShapeDtypeStruct((b, s, D_MODEL), F32),
        scratch_shapes=[pltpu.VMEM((N_PAIRS, tm, LANES), F32),
                        pltpu.VMEM((len(ms) + len(ls), 1, tm, LANES), F32)],
        compiler_params=pltpu.CompilerParams(dimension_semantics=("arbitrary", "arbitrary")),
        name="merge_proj",
    )(x, g, wg, *os_, *ms, *ls, ob, pa, pb, wo, expand)


def _ffn_kernel(x_ref, g_ref, wg_ref, wu_ref, wd_ref, gf_ref, out_ref, *, final_norm):
    x = x_ref[...]
    h = _rmsnorm(x, g_ref[...]).astype(BF16)
    act = (jax.nn.silu(_dot(h, wg_ref[...])) * _dot(h, wu_ref[...])).astype(BF16)
    x2 = x + _dot(act, wd_ref[...])
    out_ref[...] = _rmsnorm(x2, gf_ref[...]) if final_norm else x2


def _ffn_call(x2, g, wg, wu, wd, gf, final_norm, tm=512):
    t = x2.shape[0]
    row = lambda i: (i, 0)
    const = lambda i: (0, 0)
    vec = pl.BlockSpec((1, D_MODEL), const)
    resident = lambda shape: pl.BlockSpec(shape, const, pipeline_mode=pl.Buffered(1))
    return pl.pallas_call(
        functools.partial(_ffn_kernel, final_norm=final_norm),
        grid=(t // tm,),
        in_specs=[pl.BlockSpec((tm, D_MODEL), row), vec,
                  resident((D_MODEL, D_FF)), resident((D_MODEL, D_FF)),
                  resident((D_FF, D_MODEL)), vec],
        out_specs=pl.BlockSpec((tm, D_MODEL), row),
        out_shape=jax.ShapeDtypeStruct((t, D_MODEL), F32),
        compiler_params=pltpu.CompilerParams(dimension_semantics=("arbitrary",)),
        name="ffn_final",
    )(x2, g, wg, wu, wd, gf)


def _rotary_tables(seq):
    half = ROPE_DIM // 2
    inv_freq = ROPE_THETA ** (-jnp.arange(half, dtype=F32) * (2.0 / ROPE_DIM))
    ang = jnp.arange(seq, dtype=jnp.int32).astype(F32)[:, None] * inv_freq[None, :]
    cos, sin = jnp.cos(ang), jnp.sin(ang)
    zeros = lambda n: jnp.zeros((seq, n), F32)
    rest = HEAD_DIM - ROPE_DIM
    per_head = lambda parts: jnp.tile(jnp.concatenate(parts, axis=1), (1, LANES // HEAD_DIM))
    c = per_head([cos, cos, jnp.ones((seq, rest), F32)])
    s1 = per_head([zeros(half), sin, zeros(rest)])
    s2 = per_head([-sin, zeros(half), zeros(rest)])
    return c, s1, s2


def kernel(x, norm_mix_g, w_in, shift_mu, decay_w0, decay_w2, iclr_a0, iclr_a2, gate_g2, k_k, k_a, r_k, ln_x_w, ln_x_b, proj_attn, proj_rwkv, w_out, norm_ffn_g, ffn_w_gate, ffn_w_up, ffn_w_down, norm_final_g):
    b, s, d = x.shape
    assert d == D_MODEL and s % (16 * 256) == 0
    depth = w_in.shape[0]
    c, s1, s2 = _rotary_tables(s)
    vec = lambda a: a.reshape(1, -1).astype(F32)
    n_attn = 3 * D_MODEL
    for l in range(depth):
        g_mix = vec(norm_mix_g[l])
        w = w_in[l]
        qkv = _qkv_call(x, g_mix, w[:, :n_attn].astype(BF16), c, s1, s2)
        zero = jnp.zeros((DECAY_LORA, D_MODEL), F32)
        w2a2 = jnp.concatenate(
            [jnp.concatenate([decay_w2[l], zero], axis=1),
             jnp.concatenate([zero, iclr_a2[l]], axis=1)], axis=0).astype(BF16)
        feats = _rwkv_feat_call(x, g_mix, w[:, n_attn:n_attn + D_SHIFTED].astype(BF16),
                                vec(shift_mu[l]), vec(decay_w0[l]), w2a2, vec(iclr_a0[l]),
                                gate_g2[l].astype(BF16), vec(k_k[l]), vec(k_a[l]), vec(r_k[l]))
        os_, ms, ls = [], [], []
        for gi in range(len(DILATED_GROUPS)):
            o_g, m_g, l_g = _attn_call(*qkv[3 * gi:3 * gi + 3])
            os_.append(o_g)
            ms.append(m_g)
            ls.append(l_g)
        o_b = _rwkv_call(feats, vec(ln_x_w[l]), vec(ln_x_b[l]))
        x = _merge_call(x, g_mix, w[:, n_attn + D_SHIFTED:].astype(BF16), os_, ms, ls, o_b,
                        proj_attn[l].astype(BF16), proj_rwkv[l].astype(BF16),
                        w_out[l].astype(BF16))
        x = _ffn_call(x.reshape(b * s, d), vec(norm_ffn_g[l]), ffn_w_gate[l].astype(BF16),
                      ffn_w_up[l].astype(BF16), ffn_w_down[l].astype(BF16),
                      vec(norm_final_g), final_norm=(l == depth - 1)).reshape(b, s, d)
    return x
```

```python
import functools

import jax
import jax.numpy as jnp
from jax import lax
from jax.experimental import pallas as pl
from jax.experimental.pallas import tpu as pltpu

F32 = jnp.float32
BF16 = jnp.bfloat16

D_MODEL = 1024
HEAD_DIM = 64
N_HEADS = 16
ROPE_DIM = 16
ROPE_THETA = 500000.0
DILATED_GROUPS = ((128, 1), (512, 4), (2048, 16))
ATTN_BLOCK = 128
DECAY_LORA = 64
ICLR_LORA = 64
GATE_LORA = 128
D_FF = 2816
RMS_EPS = 1e-6
GN_EPS = 64e-5
D_SHIFTED = 3 * D_MODEL + DECAY_LORA + ICLR_LORA + GATE_LORA

LANES = 128
SUBLANES = 8
N_PAIRS = D_MODEL // LANES
CHUNK = 64
GROUP = 256

LOG2_E = 1.4426950408889634
NT_DIMS = (((1,), (1,)), ((), ()))


def _dot(a, b):
    return jnp.dot(a, b, preferred_element_type=F32)


def _dot_nt(a, b):
    return lax.dot_general(a, b, NT_DIMS, preferred_element_type=F32)


def _rmsnorm(x, g):
    return x * lax.rsqrt(jnp.mean(x * x, axis=-1, keepdims=True) + RMS_EPS) * g


def _qkv_kernel(x_ref, g_ref, w_ref, c_ref, s1_ref, s2_ref, *refs, tm):
    outs, stages = refs[:-2], refs[-2:]
    dilations = [d for _, d in DILATED_GROUPS]
    assert dilations[0] == 1
    h = _rmsnorm(x_ref[0], g_ref[...]).astype(BF16)
    p = _dot(h, w_ref[...])
    c, s1, s2 = c_ref[...], s1_ref[...], s2_ref[...]

    def rot(t):
        return t * c + pltpu.roll(t, 8, 1) * s1 + pltpu.roll(t, LANES - 8, 1) * s2

    for blk in range(N_PAIRS):
        lanes = slice(blk * LANES, (blk + 1) * LANES)
        vals = (rot(p[:, lanes]) * (HEAD_DIM ** -0.5 * LOG2_E),
                rot(p[:, D_MODEL + blk * LANES:D_MODEL + (blk + 1) * LANES]),
                p[:, 2 * D_MODEL + blk * LANES:2 * D_MODEL + (blk + 1) * LANES])
        for j, val in enumerate(vals):
            stages[0][j * N_PAIRS + blk] = val
            outs[j][0, 0, :, lanes] = val.astype(BF16)

    d_prev = 1
    for gi in range(1, len(dilations)):
        d = dilations[gi]
        step = d // d_prev
        assert step * d_prev == d
        rows_prev, rows = tm // d_prev, tm // d
        src_ref, dst_ref = stages[(gi - 1) % 2], stages[gi % 2]
        for j in range(3):
            for blk in range(N_PAIRS):
                slab = j * N_PAIRS + blk
                lanes = slice(blk * LANES, (blk + 1) * LANES)
                for r_prev in range(d_prev):
                    for off in range(step):
                        r = d_prev * off + r_prev
                        val = src_ref[slab, pl.ds(r_prev * rows_prev + off, rows, stride=step), :]
                        outs[3 * gi + j][0, r, :, lanes] = val.astype(BF16)
                        if gi + 1 < len(dilations):
                            dst_ref[slab, r * rows:(r + 1) * rows, :] = val
        d_prev = d


def _qkv_call(x, g, w, c, s1, s2, tm=512):
    b, s, _ = x.shape
    const = lambda bb, i: (0, 0)
    pos = lambda bb, i: (i, 0)
    out_specs, out_shape = [], []
    for _, d in DILATED_GROUPS:
        for _j in range(3):
            out_specs.append(pl.BlockSpec((1, d, tm // d, D_MODEL), lambda bb, i: (bb, 0, i, 0)))
            out_shape.append(jax.ShapeDtypeStruct((b, d, s // d, D_MODEL), BF16))
    return pl.pallas_call(
        functools.partial(_qkv_kernel, tm=tm),
        grid=(b, s // tm),
        in_specs=[pl.BlockSpec((1, tm, D_MODEL), lambda bb, i: (bb, i, 0)),
                  pl.BlockSpec((1, D_MODEL), const),
                  pl.BlockSpec((D_MODEL, 3 * D_MODEL), const),
                  pl.BlockSpec((tm, LANES), pos), pl.BlockSpec((tm, LANES), pos),
                  pl.BlockSpec((tm, LANES), pos)],
        out_specs=out_specs,
        out_shape=out_shape,
        scratch_shapes=[pltpu.VMEM((3 * N_PAIRS, tm, LANES), F32),
                        pltpu.VMEM((3 * N_PAIRS, tm, LANES), F32)],
        compiler_params=pltpu.CompilerParams(dimension_semantics=("arbitrary", "arbitrary")),
        name="qkv_proj",
    )(x, g, w, c, s1, s2)


def _rwkv_feat_kernel(x_ref, g_ref, w_ref, mu_ref, w0_ref, w2a2_ref, a0_ref, g2_ref, kk_ref,
                      ka_ref, rk_ref, ones_ref,
                      at_ref, bt_ref, kt_ref, vb_ref, lt_ref, rt_ref, gate_ref, bonus_ref, pc_ref,
                      carry_ref, *, tm):
    @pl.when(pl.program_id(1) == 0)
    def _():
        carry_ref[...] = jnp.zeros_like(carry_ref)

    lane = lax.broadcasted_iota(jnp.int32, (1, LANES), 1)
    head0 = lane < HEAD_DIM
    ones_bd = ones_ref[...]

    h = _rmsnorm(x_ref[0], g_ref[...]).astype(BF16)
    cols = _dot(h, w_ref[...])
    row8 = lax.broadcasted_iota(jnp.int32, (SUBLANES, 1), 0)
    carry = carry_ref[...]
    carry_ref[...] = cols[tm - 1:tm, :]
    mu = mu_ref[...]
    one_minus_mu = 1.0 - mu

    def shifted(ci, lanes):
        cur = cols[ci * CHUNK:(ci + 1) * CHUNK, lanes]
        last = carry[:, lanes] if ci == 0 else cols[ci * CHUNK - 1:ci * CHUNK, lanes]
        prev = pltpu.roll(cur, 1, 0)
        prev = jnp.concatenate([jnp.where(row8 == 0, last, prev[:SUBLANES]),
                                prev[SUBLANES:]], axis=0)
        return cur * one_minus_mu[:, lanes] + prev * mu[:, lanes]

    def prefix_sum(x):
        outs, total = [], None
        for gi in range(x.shape[0] // SUBLANES):
            xg = x[gi * SUBLANES:(gi + 1) * SUBLANES]
            for sh in (1, 2, 4):
                xg = xg + jnp.where(row8 >= sh, pltpu.roll(xg, sh, 0), 0.0)
            if total is not None:
                xg = xg + total
            total = xg[SUBLANES - 1:SUBLANES]
            outs.append(xg)
        return jnp.concatenate(outs, axis=0)

    w0, a0, k_k, k_a, r_k = (ref[...] for ref in (w0_ref, a0_ref, kk_ref, ka_ref, rk_ref))
    pairs_per_group = GROUP // LANES
    for ci in range(tm // CHUNK):
        rows = slice(ci * CHUNK, (ci + 1) * CHUNK)
        tail = shifted(ci, slice(3 * D_MODEL, D_SHIFTED))
        slab, g_lo = tail[:, :LANES], tail[:, LANES:]
        z = jnp.where(head0, jnp.tanh(slab), slab).astype(BF16)
        lora = _dot(z, w2a2_ref[...])
        gate_ref[0, rows, :] = _dot(jax.nn.sigmoid(g_lo).astype(BF16), g2_ref[...])
        for gi in range(D_MODEL // GROUP):
            lanes = slice(gi * GROUP, (gi + 1) * GROUP)
            at = lambda base: slice(base + gi * GROUP, base + (gi + 1) * GROUP)
            r, k, v = shifted(ci, at(0)), shifted(ci, at(D_MODEL)), shifted(ci, at(2 * D_MODEL))
            lw = jax.nn.sigmoid(w0[:, lanes] + lora[:, lanes]) * (-(jnp.e ** -0.5))
            eta = jax.nn.sigmoid(a0[:, lanes] + lora[:, at(D_MODEL)])
            kk = k * k_k[:, lanes]
            kk = kk * jnp.minimum(lax.rsqrt(_headsum(kk * kk, ones_bd)), 1e12)
            k_mod = k * (1.0 + (eta - 1.0) * k_a[:, lanes])
            bonus_ref[0, rows, lanes] = _headsum(r * k_mod * r_k[:, lanes], ones_bd) * v
            b_s = kk * eta

            cl = prefix_sum(lw)
            cl_end = cl[CHUNK - 1:CHUNK, :]
            e_ncl = jnp.exp(-cl)
            e_end = jnp.exp(cl_end - cl)
            at_ref[0, rows, lanes] = (-kk * jnp.exp(cl - lw)).astype(BF16)
            rt_ref[0, rows, lanes] = r * jnp.exp(cl)
            bt_ref[0, rows, lanes] = (b_s * e_ncl).astype(BF16)
            kt_ref[0, rows, lanes] = (k_mod * e_ncl).astype(BF16)
            vb_ref[0, rows, lanes] = v.astype(BF16)
            pc_ref[0, ci, :, lanes] = jnp.exp(cl_end)
            b_h = b_s * e_end
            k_h = k_mod * e_end
            for pp in range(pairs_per_group):
                sl = slice(pp * LANES, (pp + 1) * LANES)
                lt_ref[0, ci, gi * pairs_per_group + pp] = jnp.concatenate(
                    [b_h[:, sl], k_h[:, sl]], axis=0).T.astype(BF16)


def _headsum(x, ones_bd):
    parts = []
    for gi in range(x.shape[1] // GROUP):
        xg = x[:, gi * GROUP:(gi + 1) * GROUP]
        hi = xg.astype(BF16)
        lo = (xg - hi.astype(F32)).astype(BF16)
        parts.append(_dot(hi, ones_bd) + _dot(lo, ones_bd))
    return jnp.concatenate(parts, axis=1)


def _head_ones():
    return jnp.kron(jnp.eye(GROUP // HEAD_DIM, dtype=F32),
                    jnp.ones((HEAD_DIM, HEAD_DIM), F32)).astype(BF16)


def _rwkv_feat_call(x, g, w, mu, w0, w2a2, a0, g2, k_k, k_a, r_k, tm=256):
    b, s, _ = x.shape
    n_chunks = tm // CHUNK
    const = lambda bb, i: (0, 0)
    vec = pl.BlockSpec((1, D_MODEL), const)
    tok = lambda bb, i: (bb, i, 0)
    tok_spec = pl.BlockSpec((1, tm, D_MODEL), tok)
    tok_shape = lambda dt: jax.ShapeDtypeStruct((b, s, D_MODEL), dt)
    return pl.pallas_call(
        functools.partial(_rwkv_feat_kernel, tm=tm),
        grid=(b, s // tm),
        in_specs=[tok_spec, vec,
                  pl.BlockSpec((D_MODEL, D_SHIFTED), const),
                  pl.BlockSpec((1, D_SHIFTED), const), vec,
                  pl.BlockSpec((LANES, 2 * D_MODEL), const), vec,
                  pl.BlockSpec((GATE_LORA, D_MODEL), const), vec, vec, vec,
                  pl.BlockSpec((GROUP, GROUP), const)],
        out_specs=[tok_spec, tok_spec, tok_spec, tok_spec,
                   pl.BlockSpec((1, n_chunks, N_PAIRS, LANES, LANES),
                                lambda bb, i: (bb, i, 0, 0, 0)),
                   tok_spec, tok_spec, tok_spec,
                   pl.BlockSpec((1, n_chunks, 1, D_MODEL), lambda bb, i: (bb, i, 0, 0))],
        out_shape=[tok_shape(BF16), tok_shape(BF16), tok_shape(BF16), tok_shape(BF16),
                   jax.ShapeDtypeStruct((b, s // CHUNK, N_PAIRS, LANES, LANES), BF16),
                   tok_shape(F32), tok_shape(F32), tok_shape(F32),
                   jax.ShapeDtypeStruct((b, s // CHUNK, 1, D_MODEL), F32)],
        scratch_shapes=[pltpu.VMEM((1, D_SHIFTED), F32)],
        compiler_params=pltpu.CompilerParams(dimension_semantics=("arbitrary", "arbitrary")),
        name="rwkv_feat",
    )(x, g, w, mu, w0, w2a2, a0, g2, k_k, k_a, r_k, _head_ones())


def _attn_kernel(q_ref, kc_ref, kp_ref, vc_ref, vp_ref, o_ref, m_ref, l_ref, *, tq):
    n = pl.program_id(2)
    blk = ATTN_BLOCK
    qi = lax.broadcasted_iota(jnp.int32, (blk, 1), 0)
    kc = lax.broadcasted_iota(jnp.int32, (1, blk), 1)
    upper = kc > qi
    diag_f = (kc == qi).astype(F32)
    upper_b = upper.astype(BF16)
    lower_b = (kc <= qi).astype(BF16)
    diag_b = diag_f.astype(BF16)
    lane = lax.broadcasted_iota(jnp.int32, (1, LANES), 1)
    head0 = lane < HEAD_DIM
    ones_v = jnp.ones((2 * blk, LANES), BF16)

    for i in range(tq // blk):
        r0 = i * blk
        m_tile = jnp.zeros((blk, LANES), F32)
        l_tile = jnp.ones((blk, LANES), F32)
        for hp in range(N_PAIRS):
            l0 = hp * LANES
            qs = q_ref[r0:r0 + blk, l0:l0 + LANES]
            if i == 0:
                kprev = kp_ref[:, l0:l0 + LANES]
                vprev = vp_ref[:, l0:l0 + LANES]
            else:
                kprev = kc_ref[r0 - blk:r0, l0:l0 + LANES]
                vprev = vc_ref[r0 - blk:r0, l0:l0 + LANES]
            k2 = jnp.concatenate([kprev, kc_ref[r0:r0 + blk, l0:l0 + LANES]], axis=0)
            v2 = jnp.concatenate([vprev, vc_ref[r0:r0 + blk, l0:l0 + LANES]], axis=0)
            zq = jnp.zeros_like(qs)
            q_st = jnp.concatenate([jnp.where(head0, qs, zq), jnp.where(head0, zq, qs)], axis=0)
            s = _dot_nt(q_st, k2)
            ps, ms = [], []
            for hh in range(2):
                s_prev = s[hh * blk:(hh + 1) * blk, :blk]
                s_cur = s[hh * blk:(hh + 1) * blk, blk:]
                if i == 0:
                    s_prev = jnp.where(n > 0, s_prev, -1e30)
                s_far = jnp.sum(s_prev * diag_f, axis=1, keepdims=True)
                s_tile = jnp.where(upper, s_prev, s_cur)
                m = jnp.maximum(jnp.max(s_tile, axis=1, keepdims=True), s_far)
                p = jnp.exp2(s_tile - m).astype(BF16)
                p_far = jnp.exp2(s_far - m).astype(BF16)
                ps.append(jnp.concatenate([p * upper_b + p_far * diag_b, p * lower_b], axis=1))
                ms.append(m)
            o2 = _dot(jnp.concatenate(ps, axis=0),
                      jnp.concatenate([v2, ones_v], axis=1))
            o = jnp.where(head0, o2[:blk, :LANES], o2[blk:, :LANES])
            o_ref[r0:r0 + blk, l0:l0 + LANES] = o.astype(BF16)
            for hh in range(2):
                mine = lane == 2 * hp + hh
                m_tile = jnp.where(mine, ms[hh], m_tile)
                l_tile = jnp.where(mine, o2[hh * blk:(hh + 1) * blk, LANES:], l_tile)
        m_ref[r0:r0 + blk, :] = m_tile
        l_ref[r0:r0 + blk, :] = l_tile


def _attn_call(q, k, v, tq=1024):
    b, d, sub, _ = q.shape
    cur = lambda bb, r, n: (bb, r, n, 0)
    prev = lambda bb, r, n: (bb, r, jnp.maximum(n * (tq // ATTN_BLOCK) - 1, 0), 0)
    stat = jax.ShapeDtypeStruct((b, d, sub, LANES), F32)
    return pl.pallas_call(
        functools.partial(_attn_kernel, tq=tq),
        grid=(b, d, sub // tq),
        in_specs=[pl.BlockSpec((None, None, tq, D_MODEL), cur),
                  pl.BlockSpec((None, None, tq, D_MODEL), cur),
                  pl.BlockSpec((None, None, ATTN_BLOCK, D_MODEL), prev),
                  pl.BlockSpec((None, None, tq, D_MODEL), cur),
                  pl.BlockSpec((None, None, ATTN_BLOCK, D_MODEL), prev)],
        out_specs=[pl.BlockSpec((None, None, tq, D_MODEL), cur),
                   pl.BlockSpec((None, None, tq, LANES), cur),
                   pl.BlockSpec((None, None, tq, LANES), cur)],
        out_shape=[jax.ShapeDtypeStruct((b, d, sub, D_MODEL), BF16), stat, stat],
        compiler_params=pltpu.CompilerParams(
            dimension_semantics=("arbitrary", "arbitrary", "arbitrary")),
        name=f"dilated_attn_d{d}",
    )(q, k, k, v, v)


def _rwkv_kernel(at_ref, bt_ref, kt_ref, vb_ref, lt_ref, rt_ref, gate_ref, bonus_ref, pc_ref,
                 lnw_ref, lnb_ref, ones_ref, o_ref, h_ref):
    c_len = CHUNK

    @pl.when(pl.program_id(1) == 0)
    def _():
        h_ref[...] = jnp.zeros_like(h_ref)

    row = lax.broadcasted_iota(jnp.int32, (c_len, 1), 0)
    lane = lax.broadcasted_iota(jnp.int32, (1, LANES), 1)
    head0 = lane < HEAD_DIM

    col = lane % HEAD_DIM
    strict = row > col
    incl = row >= col
    eye = (row == col).astype(F32)
    n_rows = at_ref.shape[0]
    n_chunks = at_ref.shape[1] // c_len
    chains = [(ci, bb, p) for ci in range(n_chunks) for bb in range(n_rows)
              for p in range(N_PAIRS)]
    cut = lambda ref: [ref[bb, ci * c_len:(ci + 1) * c_len, p * LANES:(p + 1) * LANES]
                       for ci, bb, p in chains]
    at, rt, bt, kt, vb = (cut(ref) for ref in (at_ref, rt_ref, bt_ref, kt_ref, vb_ref))
    pcs = [pc_ref[bb, ci, :, p * LANES:(p + 1) * LANES] for ci, bb, p in chains]
    top = lambda ts: [t[:c_len] for t in ts]
    bot = lambda ts: [t[c_len:] for t in ts]
    stack = lambda xs, ys: [jnp.concatenate([x, y], axis=0) for x, y in zip(xs, ys)]
    to_bf16 = lambda ts: [t.astype(BF16) for t in ts]

    def bd(y):
        zz = jnp.zeros_like(y)
        return jnp.concatenate([jnp.where(head0, y, zz), jnp.where(head0, zz, y)], axis=0)

    def pmm(xs, ys):
        return [_dot(x, bd(y)) for x, y in zip(xs, ys)]

    a_all = [_dot_nt(lhs, jnp.concatenate([bd(b), bd(kk_)], axis=0))
             for lhs, b, kk_ in zip(stack(at, to_bf16(rt)), bt, kt)]
    l_ab = [jnp.where(strict, a[:c_len, :LANES], 0.0) for a in a_all]
    a_ak = [jnp.where(strict, a[:c_len, LANES:], 0.0).astype(BF16) for a in a_all]
    a_rb = [jnp.where(incl, a[c_len:, :LANES], 0.0).astype(BF16) for a in a_all]
    a_rk = [jnp.where(incl, a[c_len:, LANES:], 0.0).astype(BF16) for a in a_all]

    l_b = to_bf16(l_ab)
    s_acc = [eye + l for l in l_ab]
    q_b = to_bf16(pmm(l_b, l_b))
    for _i in range(4):
        res = pmm(stack(to_bf16(s_acc), q_b), q_b)
        s_acc = [s + r_ for s, r_ in zip(s_acc, top(res))]
        q_b = to_bf16(bot(res))
    s_acc = [s + r_ for s, r_ in zip(s_acc, pmm(to_bf16(s_acc), q_b))]
    t_b = to_bf16(s_acc)

    res = pmm(stack(a_ak, a_rk), vb)
    x2_b, yv2 = to_bf16(top(res)), bot(res)
    def pmm2(xs, ys, zs):
        return [_dot(x, jnp.concatenate([bd(y), bd(z)], axis=1)) for x, y, z in zip(xs, ys, zs)]

    res = pmm2(t_b, at, x2_b)
    a_hat_b = to_bf16([r_[:, :LANES] for r_ in res])
    u_v_b = to_bf16([r_[:, LANES:] for r_ in res])
    res = pmm2(a_rb, a_hat_b, u_v_b)
    r_hat = [r_ + d[:, :LANES] for r_, d in zip(rt, res)]
    y_v = [d[:, LANES:] + y2 for d, y2 in zip(res, yv2)]

    lts = [lt_ref[bb, ci, p] for ci, bb, p in chains]
    mns = [_dot(lt, jnp.concatenate(
        [jnp.concatenate([ah, jnp.zeros_like(ah)], axis=0),
         jnp.concatenate([uv, v_], axis=0)], axis=1))
        for lt, ah, uv, v_ in zip(lts, a_hat_b, u_v_b, vb)]
    m_p = [jnp.where(head0, mn[:c_len, :LANES], mn[c_len:, :LANES]) + eye * pc_
           for mn, pc_ in zip(mns, pcs)]
    n_p = [jnp.where(head0, mn[:c_len, LANES:], mn[c_len:, LANES:]) for mn in mns]

    lhs = stack(to_bf16(m_p), to_bf16(r_hat))
    states = [h_ref[bb, p] for bb in range(n_rows) for p in range(N_PAIRS)]
    per_chunk = n_rows * N_PAIRS
    ones_bd = ones_ref[...]
    for ci in range(n_chunks):
        sel = slice(ci * per_chunk, (ci + 1) * per_chunk)
        res = pmm(lhs[sel], to_bf16(states))
        states = [r_[:c_len] + n_ for r_, n_ in zip(res, n_p[sel])]
        ys = [r_[c_len:] + yv for r_, yv in zip(res, y_v[sel])]
        rows = slice(ci * c_len, (ci + 1) * c_len)
        for bb in range(n_rows):
            y = jnp.concatenate(ys[bb * N_PAIRS:(bb + 1) * N_PAIRS], axis=1)
            mean = _headsum(y, ones_bd) * (1.0 / HEAD_DIM)
            yc = y - mean
            var = _headsum(yc * yc, ones_bd) * (1.0 / HEAD_DIM)
            yn = yc * lax.rsqrt(var + GN_EPS) * lnw_ref[...] + lnb_ref[...]
            o_ref[bb, rows, :] = ((yn + bonus_ref[bb, rows, :]) * gate_ref[bb, rows, :]).astype(BF16)
    for j, h_new in enumerate(states):
        h_ref[j // N_PAIRS, j % N_PAIRS] = h_new


def _rwkv_call(feats, ln_w, ln_b, n_chunks=2):
    at, bt, kt, vb, lt, rt, gate, bonus, pc = feats
    b, s, _ = at.shape
    nb = 2 if b % 2 == 0 else 1
    span = n_chunks * CHUNK
    const = lambda bb, t: (0, 0)
    vec = pl.BlockSpec((1, D_MODEL), const)
    tok = pl.BlockSpec((nb, span, D_MODEL), lambda bb, t: (bb, t, 0))
    return pl.pallas_call(
        _rwkv_kernel,
        grid=(b // nb, s // span),
        in_specs=[tok, tok, tok, tok,
                  pl.BlockSpec((nb, n_chunks, N_PAIRS, LANES, LANES),
                               lambda bb, t: (bb, t, 0, 0, 0)),
                  tok, tok, tok,
                  pl.BlockSpec((nb, n_chunks, 1, D_MODEL), lambda bb, t: (bb, t, 0, 0)),
                  vec, vec, pl.BlockSpec((GROUP, GROUP), const)],
        out_specs=tok,
        out_shape=jax.ShapeDtypeStruct((b, s, D_MODEL), BF16),
        scratch_shapes=[pltpu.VMEM((nb, N_PAIRS, CHUNK, LANES), F32)],
        compiler_params=pltpu.CompilerParams(dimension_semantics=("arbitrary", "arbitrary")),
        name="rwkv7_mixer",
    )(at, bt, kt, vb, lt, rt, gate, bonus, pc, ln_w, ln_b, _head_ones())


def _merge_kernel(x_ref, g_ref, wg_ref, o1_ref, o4_ref, o16_ref, m1_ref, m4_ref, m16_ref,
                  l1_ref, l4_ref, l16_ref, ob_ref, pa_ref, pb_ref, wo_ref, e_ref, out_ref,
                  o_scr, stat_scr):
    x = x_ref[0]
    h = _rmsnorm(x, g_ref[...]).astype(BF16)
    gl = _dot(h, wg_ref[...])
    g_a = jax.nn.sigmoid(gl[:, :D_MODEL])
    g_b = jax.nn.sigmoid(gl[:, D_MODEL:])

    def natural_order(src_ref, dst_ref):
        _, d, rows, width = src_ref.shape
        for r in range(d):
            for cblk in range(width // LANES):
                val = src_ref[0, r, :, cblk * LANES:(cblk + 1) * LANES].astype(F32)
                if d == 1:
                    dst_ref[cblk] = val
                else:
                    dst_ref[cblk, pl.ds(r, rows, stride=d), :] = val

    ms, ls = [], []
    for gi, (m_ref, l_ref) in enumerate(((m1_ref, l1_ref), (m4_ref, l4_ref), (m16_ref, l16_ref))):
        natural_order(m_ref, stat_scr.at[2 * gi])
        natural_order(l_ref, stat_scr.at[2 * gi + 1])
        ms.append(stat_scr[2 * gi, 0])
        ls.append(stat_scr[2 * gi + 1, 0])
    mx = jnp.maximum(jnp.maximum(ms[0], ms[1]), ms[2])
    es = [jnp.exp2(m - mx) for m in ms]
    den = es[0] * ls[0] + es[1] * ls[1] + es[2] * ls[2]
    expand = e_ref[...]
    o_a = jnp.zeros(x.shape, F32)
    for e, o_ref in zip(es, (o1_ref, o4_ref, o16_ref)):
        w = e / den
        hi = w.astype(BF16)
        lo = (w - hi.astype(F32)).astype(BF16)
        natural_order(o_ref, o_scr)
        o_g = jnp.concatenate([o_scr[p] for p in range(N_PAIRS)], axis=1)
        o_a = o_a + _dot(jnp.concatenate([hi, lo], axis=1), expand) * o_g

    merged = (g_a * _dot(o_a.astype(BF16), pa_ref[...])
              + g_b * _dot(ob_ref[0], pb_ref[...]))
    out_ref[0] = x + _dot(merged.astype(BF16), wo_ref[...])


def _merge_call(x, g, wg, os_, ms, ls, ob, pa, pb, wo, tm=256):
    b, s, _ = x.shape
    expand = jnp.kron(jnp.eye(N_HEADS, dtype=F32), jnp.ones((1, HEAD_DIM), F32))
    expand = jnp.concatenate([expand, jnp.zeros((LANES - N_HEADS, D_MODEL), F32)], 0).astype(BF16)
    expand = jnp.concatenate([expand, expand], axis=0)
    const = lambda bb, i: (0, 0)
    wide = pl.BlockSpec((1, tm, D_MODEL), lambda bb, i: (bb, i, 0))
    sq = pl.BlockSpec((D_MODEL, D_MODEL), const)
    res = lambda a: pl.BlockSpec((1, a.shape[1], tm // a.shape[1], a.shape[3]),
                                 lambda bb, i: (bb, 0, i, 0))
    return pl.pallas_call(
        _merge_kernel,
        grid=(b, s // tm),
        in_specs=[wide, pl.BlockSpec((1, D_MODEL), const),
                  pl.BlockSpec((D_MODEL, 2 * D_MODEL), const),
                  *[res(a) for a in (*os_, *ms, *ls)], wide, sq, sq, sq,
                  pl.BlockSpec((2 * LANES, D_MODEL), const)],
        out_specs=wide,
        out_shape=jax.ShapeDtypeStruct((b, s, D_MODEL), F32),
        scratch_shapes=[pltpu.VMEM((N_PAIRS, tm, LANES), F32),
                        pltpu.VMEM((len(ms) + len(ls), 1, tm, LANES), F32)],
        compiler_params=pltpu.CompilerParams(dimension_semantics=("arbitrary", "arbitrary")),
        name="merge_proj",
    )(x, g, wg, *os_, *ms, *ls, ob, pa, pb, wo, expand)


def _ffn_kernel(x_ref, g_ref, wg_ref, wu_ref, wd_ref, gf_ref, out_ref, *, final_norm):
    x = x_ref[...]
    h = _rmsnorm(x, g_ref[...]).astype(BF16)
    act = (jax.nn.silu(_dot(h, wg_ref[...])) * _dot(h, wu_ref[...])).astype(BF16)
    x2 = x + _dot(act, wd_ref[...])
    out_ref[...] = _rmsnorm(x2, gf_ref[...]) if final_norm else x2


def _ffn_call(x2, g, wg, wu, wd, gf, final_norm, tm=512):
    t = x2.shape[0]
    row = lambda i: (i, 0)
    const = lambda i: (0, 0)
    vec = pl.BlockSpec((1, D_MODEL), const)
    resident = lambda shape: pl.BlockSpec(shape, const, pipeline_mode=pl.Buffered(1))
    return pl.pallas_call(
        functools.partial(_ffn_kernel, final_norm=final_norm),
        grid=(t // tm,),
        in_specs=[pl.BlockSpec((tm, D_MODEL), row), vec,
                  resident((D_MODEL, D_FF)), resident((D_MODEL, D_FF)),
                  resident((D_FF, D_MODEL)), vec],
        out_specs=pl.BlockSpec((tm, D_MODEL), row),
        out_shape=jax.ShapeDtypeStruct((t, D_MODEL), F32),
        compiler_params=pltpu.CompilerParams(dimension_semantics=("arbitrary",)),
        name="ffn_final",
    )(x2, g, wg, wu, wd, gf)


def _rotary_tables(seq):
    half = ROPE_DIM // 2
    in_head = jnp.arange(LANES, dtype=jnp.int32) % HEAD_DIM
    freq = (in_head % half).astype(F32)
    inv_freq = ROPE_THETA ** (-freq * (2.0 / ROPE_DIM))
    ang = jnp.arange(seq, dtype=jnp.int32).astype(F32)[:, None] * inv_freq[None, :]
    cos, sin = jnp.cos(ang), jnp.sin(ang)
    first = (in_head < half)[None, :]
    second = ((in_head >= half) & (in_head < ROPE_DIM))[None, :]
    c = jnp.where(first | second, cos, 1.0)
    s1 = jnp.where(second, sin, 0.0)
    s2 = jnp.where(first, -sin, 0.0)
    return c, s1, s2


def kernel(x, norm_mix_g, w_in, shift_mu, decay_w0, decay_w2, iclr_a0, iclr_a2, gate_g2, k_k, k_a, r_k, ln_x_w, ln_x_b, proj_attn, proj_rwkv, w_out, norm_ffn_g, ffn_w_gate, ffn_w_up, ffn_w_down, norm_final_g):
    b, s, d = x.shape
    assert d == D_MODEL and s % (max(dil for _, dil in DILATED_GROUPS) * 1024) == 0
    depth = w_in.shape[0]
    c, s1, s2 = _rotary_tables(s)
    vec = lambda a: a.reshape(1, -1).astype(F32)
    n_attn = 3 * D_MODEL
    for l in range(depth):
        g_mix = vec(norm_mix_g[l])
        w = w_in[l]
        qkv = _qkv_call(x, g_mix, w[:, :n_attn].astype(BF16), c, s1, s2)
        zero = jnp.zeros((DECAY_LORA, D_MODEL), F32)
        w2a2 = jnp.concatenate(
            [jnp.concatenate([decay_w2[l], zero], axis=1),
             jnp.concatenate([zero, iclr_a2[l]], axis=1)], axis=0).astype(BF16)
        feats = _rwkv_feat_call(x, g_mix, w[:, n_attn:n_attn + D_SHIFTED].astype(BF16),
                                vec(shift_mu[l]), vec(decay_w0[l]), w2a2, vec(iclr_a0[l]),
                                gate_g2[l].astype(BF16), vec(k_k[l]), vec(k_a[l]), vec(r_k[l]))
        os_, ms, ls = [], [], []
        for gi in range(len(DILATED_GROUPS)):
            o_g, m_g, l_g = _attn_call(*qkv[3 * gi:3 * gi + 3])
            os_.append(o_g)
            ms.append(m_g)
            ls.append(l_g)
        o_b = _rwkv_call(feats, vec(ln_x_w[l]), vec(ln_x_b[l]))
        x = _merge_call(x, g_mix, w[:, n_attn + D_SHIFTED:].astype(BF16), os_, ms, ls, o_b,
                        proj_attn[l].astype(BF16), proj_rwkv[l].astype(BF16),
                        w_out[l].astype(BF16))
        x = _ffn_call(x.reshape(b * s, d), vec(norm_ffn_g[l]), ffn_w_gate[l].astype(BF16),
                      ffn_w_up[l].astype(BF16), ffn_w_down[l].astype(BF16),
                      vec(norm_final_g), final_norm=(l == depth - 1)).reshape(b, s, d)
    return x
```

```python
import functools

import jax
import jax.numpy as jnp
import numpy as np
from jax import lax
from jax.experimental import pallas as pl
from jax.experimental.pallas import tpu as pltpu

F32 = jnp.float32
BF16 = jnp.bfloat16

D_MODEL = 1024
HEAD_DIM = 64
N_HEADS = 16
ROPE_DIM = 16
ROPE_THETA = 500000.0
DILATED_GROUPS = ((128, 1), (512, 4), (2048, 16))
ATTN_BLOCK = 128
DECAY_LORA = 64
ICLR_LORA = 64
GATE_LORA = 128
D_FF = 2816
RMS_EPS = 1e-6
GN_EPS = 64e-5
D_SHIFTED = 3 * D_MODEL + DECAY_LORA + ICLR_LORA + GATE_LORA

LANES = 128
SUBLANES = 8
N_PAIRS = D_MODEL // LANES
CHUNK = 64
GROUP = 256

LOG2_E = 1.4426950408889634
NT_DIMS = (((1,), (1,)), ((), ()))


def _dot(a, b):
    return jnp.dot(a, b, preferred_element_type=F32)


def _dot_nt(a, b):
    return lax.dot_general(a, b, NT_DIMS, preferred_element_type=F32)


def _rmsnorm(x, g):
    return x * lax.rsqrt(jnp.mean(x * x, axis=-1, keepdims=True) + RMS_EPS) * g


def _qkv_kernel(x_ref, g_ref, w_ref, c_ref, s1_ref, s2_ref, *refs, tm):
    outs, stages = refs[:-2], refs[-2:]
    dilations = [d for _, d in DILATED_GROUPS]
    assert dilations[0] == 1
    h = _rmsnorm(x_ref[0], g_ref[...]).astype(BF16)
    p = _dot(h, w_ref[...])
    c, s1, s2 = c_ref[...], s1_ref[...], s2_ref[...]

    def rot(t):
        return t * c + pltpu.roll(t, 8, 1) * s1 + pltpu.roll(t, LANES - 8, 1) * s2

    for blk in range(N_PAIRS):
        lanes = slice(blk * LANES, (blk + 1) * LANES)
        vals = (rot(p[:, lanes]) * (HEAD_DIM ** -0.5 * LOG2_E),
                rot(p[:, D_MODEL + blk * LANES:D_MODEL + (blk + 1) * LANES]),
                p[:, 2 * D_MODEL + blk * LANES:2 * D_MODEL + (blk + 1) * LANES])
        for j, val in enumerate(vals):
            stages[0][j * N_PAIRS + blk] = val
            outs[j][0, 0, :, lanes] = val.astype(BF16)

    d_prev = 1
    for gi in range(1, len(dilations)):
        d = dilations[gi]
        step = d // d_prev
        assert step * d_prev == d
        rows_prev, rows = tm // d_prev, tm // d
        src_ref, dst_ref = stages[(gi - 1) % 2], stages[gi % 2]
        for j in range(3):
            for blk in range(N_PAIRS):
                slab = j * N_PAIRS + blk
                lanes = slice(blk * LANES, (blk + 1) * LANES)
                for r_prev in range(d_prev):
                    for off in range(step):
                        r = d_prev * off + r_prev
                        val = src_ref[slab, pl.ds(r_prev * rows_prev + off, rows, stride=step), :]
                        outs[3 * gi + j][0, r, :, lanes] = val.astype(BF16)
                        if gi + 1 < len(dilations):
                            dst_ref[slab, r * rows:(r + 1) * rows, :] = val
        d_prev = d


def _qkv_call(x, g, w, c, s1, s2, tm=512):
    b, s, _ = x.shape
    const = lambda bb, i: (0, 0)
    pos = lambda bb, i: (i, 0)
    out_specs, out_shape = [], []
    for _, d in DILATED_GROUPS:
        for _j in range(3):
            out_specs.append(pl.BlockSpec((1, d, tm // d, D_MODEL), lambda bb, i: (bb, 0, i, 0)))
            out_shape.append(jax.ShapeDtypeStruct((b, d, s // d, D_MODEL), BF16))
    return pl.pallas_call(
        functools.partial(_qkv_kernel, tm=tm),
        grid=(b, s // tm),
        in_specs=[pl.BlockSpec((1, tm, D_MODEL), lambda bb, i: (bb, i, 0)),
                  pl.BlockSpec((1, D_MODEL), const),
                  pl.BlockSpec((D_MODEL, 3 * D_MODEL), const),
                  pl.BlockSpec((tm, LANES), pos), pl.BlockSpec((tm, LANES), pos),
                  pl.BlockSpec((tm, LANES), pos)],
        out_specs=out_specs,
        out_shape=out_shape,
        scratch_shapes=[pltpu.VMEM((3 * N_PAIRS, tm, LANES), F32),
                        pltpu.VMEM((3 * N_PAIRS, tm, LANES), F32)],
        compiler_params=pltpu.CompilerParams(dimension_semantics=("arbitrary", "arbitrary")),
        name="qkv_proj",
    )(x, g, w, c, s1, s2)


def _rwkv_feat_kernel(x_ref, g_ref, w_ref, mu_ref, w0_ref, w2a2_ref, a0_ref, g2_ref, kk_ref,
                      ka_ref, rk_ref, ones_ref,
                      at_ref, vb_ref, lt_ref, rt_ref, gate_ref, bonus_ref, pc_ref,
                      carry_ref, *, tm):
    @pl.when(pl.program_id(1) == 0)
    def _():
        carry_ref[...] = jnp.zeros_like(carry_ref)

    lane = lax.broadcasted_iota(jnp.int32, (1, LANES), 1)
    head0 = lane < HEAD_DIM
    ones_bd = ones_ref[...]

    h = _rmsnorm(x_ref[0], g_ref[...]).astype(BF16)
    cols = _dot(h, w_ref[...])
    row8 = lax.broadcasted_iota(jnp.int32, (SUBLANES, 1), 0)
    carry = carry_ref[...]
    carry_ref[...] = cols[tm - 1:tm, :]
    mu = mu_ref[...]
    one_minus_mu = 1.0 - mu

    def shifted(ci, lanes):
        cur = cols[ci * CHUNK:(ci + 1) * CHUNK, lanes]
        last = carry[:, lanes] if ci == 0 else cols[ci * CHUNK - 1:ci * CHUNK, lanes]
        prev = pltpu.roll(cur, 1, 0)
        prev = jnp.concatenate([jnp.where(row8 == 0, last, prev[:SUBLANES]),
                                prev[SUBLANES:]], axis=0)
        return cur * one_minus_mu[:, lanes] + prev * mu[:, lanes]

    def prefix_sum(x):
        outs, total = [], None
        for gi in range(x.shape[0] // SUBLANES):
            xg = x[gi * SUBLANES:(gi + 1) * SUBLANES]
            for sh in (1, 2, 4):
                xg = xg + jnp.where(row8 >= sh, pltpu.roll(xg, sh, 0), 0.0)
            if total is not None:
                xg = xg + total
            total = xg[SUBLANES - 1:SUBLANES]
            outs.append(xg)
        return jnp.concatenate(outs, axis=0)

    w0, a0, k_k, k_a, r_k = (ref[...] for ref in (w0_ref, a0_ref, kk_ref, ka_ref, rk_ref))
    pairs_per_group = GROUP // LANES
    for ci in range(tm // CHUNK):
        rows = slice(ci * CHUNK, (ci + 1) * CHUNK)
        tail = shifted(ci, slice(3 * D_MODEL, D_SHIFTED))
        slab, g_lo = tail[:, :LANES], tail[:, LANES:]
        z = jnp.where(head0, jnp.tanh(slab), slab).astype(BF16)
        lora = _dot(z, w2a2_ref[...])
        gate_ref[0, rows, :] = _dot(jax.nn.sigmoid(g_lo).astype(BF16), g2_ref[...])
        for gi in range(D_MODEL // GROUP):
            lanes = slice(gi * GROUP, (gi + 1) * GROUP)
            at = lambda base: slice(base + gi * GROUP, base + (gi + 1) * GROUP)
            r, k, v = shifted(ci, at(0)), shifted(ci, at(D_MODEL)), shifted(ci, at(2 * D_MODEL))
            lw = jax.nn.sigmoid(w0[:, lanes] + lora[:, lanes]) * (-(jnp.e ** -0.5))
            eta = jax.nn.sigmoid(a0[:, lanes] + lora[:, at(D_MODEL)])
            kk = k * k_k[:, lanes]
            kk = kk * jnp.minimum(lax.rsqrt(_headsum(kk * kk, ones_bd)), 1e12)
            k_mod = k * (1.0 + (eta - 1.0) * k_a[:, lanes])
            bonus_ref[0, rows, lanes] = _headsum(r * k_mod * r_k[:, lanes], ones_bd) * v
            b_s = kk * eta

            cl = prefix_sum(lw)
            e_ncl = jnp.exp(-cl)
            at_ref[0, rows, lanes] = (-kk * jnp.exp(cl - lw)).astype(BF16)
            rt_ref[0, rows, lanes] = r * jnp.exp(cl)
            vb_ref[0, rows, lanes] = v.astype(BF16)
            pc_ref[0, ci, :, lanes] = jnp.exp(cl[CHUNK - 1:CHUNK, :])
            b_t = b_s * e_ncl
            k_t = k_mod * e_ncl
            for pp in range(pairs_per_group):
                sl = slice(pp * LANES, (pp + 1) * LANES)
                lt_ref[0, ci, gi * pairs_per_group + pp] = jnp.concatenate(
                    [b_t[:, sl], k_t[:, sl]], axis=0).T.astype(BF16)


def _headsum(x, ones_bd):
    parts = []
    for gi in range(x.shape[1] // GROUP):
        xg = x[:, gi * GROUP:(gi + 1) * GROUP]
        hi = xg.astype(BF16)
        lo = (xg - hi.astype(F32)).astype(BF16)
        parts.append(_dot(hi, ones_bd) + _dot(lo, ones_bd))
    return jnp.concatenate(parts, axis=1)


def _head_ones():
    return jnp.kron(jnp.eye(GROUP // HEAD_DIM, dtype=F32),
                    jnp.ones((HEAD_DIM, HEAD_DIM), F32)).astype(BF16)


def _rwkv_feat_call(x, g, w, mu, w0, w2a2, a0, g2, k_k, k_a, r_k, tm=256):
    b, s, _ = x.shape
    n_chunks = tm // CHUNK
    const = lambda bb, i: (0, 0)
    vec = pl.BlockSpec((1, D_MODEL), const)
    tok = lambda bb, i: (bb, i, 0)
    tok_spec = pl.BlockSpec((1, tm, D_MODEL), tok)
    tok_shape = lambda dt: jax.ShapeDtypeStruct((b, s, D_MODEL), dt)
    return pl.pallas_call(
        functools.partial(_rwkv_feat_kernel, tm=tm),
        grid=(b, s // tm),
        in_specs=[tok_spec, vec,
                  pl.BlockSpec((D_MODEL, D_SHIFTED), const),
                  pl.BlockSpec((1, D_SHIFTED), const), vec,
                  pl.BlockSpec((LANES, 2 * D_MODEL), const), vec,
                  pl.BlockSpec((GATE_LORA, D_MODEL), const), vec, vec, vec,
                  pl.BlockSpec((GROUP, GROUP), const)],
        out_specs=[tok_spec, tok_spec,
                   pl.BlockSpec((1, n_chunks, N_PAIRS, LANES, LANES),
                                lambda bb, i: (bb, i, 0, 0, 0)),
                   tok_spec, tok_spec, tok_spec,
                   pl.BlockSpec((1, n_chunks, 1, D_MODEL), lambda bb, i: (bb, i, 0, 0))],
        out_shape=[tok_shape(BF16), tok_shape(BF16),
                   jax.ShapeDtypeStruct((b, s // CHUNK, N_PAIRS, LANES, LANES), BF16),
                   tok_shape(F32), tok_shape(F32), tok_shape(F32),
                   jax.ShapeDtypeStruct((b, s // CHUNK, 1, D_MODEL), F32)],
        scratch_shapes=[pltpu.VMEM((1, D_SHIFTED), F32)],
        compiler_params=pltpu.CompilerParams(dimension_semantics=("arbitrary", "arbitrary")),
        name="rwkv_feat",
    )(x, g, w, mu, w0, w2a2, a0, g2, k_k, k_a, r_k, _head_ones())


def _attn_kernel(q_ref, kc_ref, kp_ref, vc_ref, vp_ref, o_ref, m_ref, l_ref, *, tq):
    n = pl.program_id(2)
    blk = ATTN_BLOCK
    qi = lax.broadcasted_iota(jnp.int32, (blk, 1), 0)
    kc = lax.broadcasted_iota(jnp.int32, (1, blk), 1)
    upper = kc > qi
    diag_f = (kc == qi).astype(F32)
    upper_b = upper.astype(BF16)
    lower_b = (kc <= qi).astype(BF16)
    diag_b = diag_f.astype(BF16)
    lane = lax.broadcasted_iota(jnp.int32, (1, LANES), 1)
    head0 = lane < HEAD_DIM
    ones_v = jnp.ones((2 * blk, LANES), BF16)

    for i in range(tq // blk):
        r0 = i * blk
        m_tile = jnp.zeros((blk, LANES), F32)
        l_tile = jnp.ones((blk, LANES), F32)
        for hp in range(N_PAIRS):
            l0 = hp * LANES
            qs = q_ref[r0:r0 + blk, l0:l0 + LANES]
            if i == 0:
                kprev = kp_ref[:, l0:l0 + LANES]
                vprev = vp_ref[:, l0:l0 + LANES]
            else:
                kprev = kc_ref[r0 - blk:r0, l0:l0 + LANES]
                vprev = vc_ref[r0 - blk:r0, l0:l0 + LANES]
            k2 = jnp.concatenate([kprev, kc_ref[r0:r0 + blk, l0:l0 + LANES]], axis=0)
            v2 = jnp.concatenate([vprev, vc_ref[r0:r0 + blk, l0:l0 + LANES]], axis=0)
            zq = jnp.zeros_like(qs)
            q_st = jnp.concatenate([jnp.where(head0, qs, zq), jnp.where(head0, zq, qs)], axis=0)
            s = _dot_nt(q_st, k2)
            ps, ms = [], []
            for hh in range(2):
                s_prev = s[hh * blk:(hh + 1) * blk, :blk]
                s_cur = s[hh * blk:(hh + 1) * blk, blk:]
                if i == 0:
                    s_prev = jnp.where(n > 0, s_prev, -1e30)
                s_far = jnp.sum(s_prev * diag_f, axis=1, keepdims=True)
                s_tile = jnp.where(upper, s_prev, s_cur)
                m = jnp.maximum(jnp.max(s_tile, axis=1, keepdims=True), s_far)
                p = jnp.exp2(s_tile - m).astype(BF16)
                p_far = jnp.exp2(s_far - m).astype(BF16)
                ps.append(jnp.concatenate([p * upper_b + p_far * diag_b, p * lower_b], axis=1))
                ms.append(m)
            o2 = _dot(jnp.concatenate(ps, axis=0),
                      jnp.concatenate([v2, ones_v], axis=1))
            o = jnp.where(head0, o2[:blk, :LANES], o2[blk:, :LANES])
            o_ref[r0:r0 + blk, l0:l0 + LANES] = o.astype(BF16)
            for hh in range(2):
                mine = lane == 2 * hp + hh
                m_tile = jnp.where(mine, ms[hh], m_tile)
                l_tile = jnp.where(mine, o2[hh * blk:(hh + 1) * blk, LANES:], l_tile)
        m_ref[r0:r0 + blk, :] = m_tile
        l_ref[r0:r0 + blk, :] = l_tile


def _attn_call(q, k, v, tq=1024):
    b, d, sub, _ = q.shape
    cur = lambda bb, r, n: (bb, r, n, 0)
    prev = lambda bb, r, n: (bb, r, jnp.maximum(n * (tq // ATTN_BLOCK) - 1, 0), 0)
    stat = jax.ShapeDtypeStruct((b, d, sub, LANES), F32)
    return pl.pallas_call(
        functools.partial(_attn_kernel, tq=tq),
        grid=(b, d, sub // tq),
        in_specs=[pl.BlockSpec((None, None, tq, D_MODEL), cur),
                  pl.BlockSpec((None, None, tq, D_MODEL), cur),
                  pl.BlockSpec((None, None, ATTN_BLOCK, D_MODEL), prev),
                  pl.BlockSpec((None, None, tq, D_MODEL), cur),
                  pl.BlockSpec((None, None, ATTN_BLOCK, D_MODEL), prev)],
        out_specs=[pl.BlockSpec((None, None, tq, D_MODEL), cur),
                   pl.BlockSpec((None, None, tq, LANES), cur),
                   pl.BlockSpec((None, None, tq, LANES), cur)],
        out_shape=[jax.ShapeDtypeStruct((b, d, sub, D_MODEL), BF16), stat, stat],
        compiler_params=pltpu.CompilerParams(
            dimension_semantics=("arbitrary", "arbitrary", "arbitrary")),
        name=f"dilated_attn_d{d}",
    )(q, k, k, v, v)


def _rwkv_kernel(at_ref, vb_ref, lt_ref, rt_ref, gate_ref, bonus_ref, pc_ref,
                 lnw_ref, lnb_ref, ones_ref, o_ref, h_ref):
    c_len = CHUNK

    @pl.when(pl.program_id(1) == 0)
    def _():
        h_ref[...] = jnp.zeros_like(h_ref)

    row = lax.broadcasted_iota(jnp.int32, (c_len, 1), 0)
    lane = lax.broadcasted_iota(jnp.int32, (1, LANES), 1)
    head0 = lane < HEAD_DIM

    col = lane % HEAD_DIM
    strict = row > col
    incl = row >= col
    eye = (row == col).astype(F32)
    n_rows = at_ref.shape[0]
    n_chunks = at_ref.shape[1] // c_len
    chains = [(ci, bb, p) for ci in range(n_chunks) for bb in range(n_rows)
              for p in range(N_PAIRS)]
    cut = lambda ref: [ref[bb, ci * c_len:(ci + 1) * c_len, p * LANES:(p + 1) * LANES]
                       for ci, bb, p in chains]
    at, rt, vb = (cut(ref) for ref in (at_ref, rt_ref, vb_ref))
    pcs = [pc_ref[bb, ci, :, p * LANES:(p + 1) * LANES] for ci, bb, p in chains]
    lts = [lt_ref[bb, ci, p] for ci, bb, p in chains]
    top = lambda ts: [t[:c_len] for t in ts]
    bot = lambda ts: [t[c_len:] for t in ts]
    stack = lambda xs, ys: [jnp.concatenate([x, y], axis=0) for x, y in zip(xs, ys)]
    to_bf16 = lambda ts: [t.astype(BF16) for t in ts]

    def bd(y):
        zz = jnp.zeros_like(y)
        return jnp.concatenate([jnp.where(head0, y, zz), jnp.where(head0, zz, y)], axis=0)

    def pmm(xs, ys):
        return [_dot(x, bd(y)) for x, y in zip(xs, ys)]

    def spread(lt):
        sw = pltpu.roll(lt, HEAD_DIM, 1)
        zz = jnp.zeros_like(lt[:c_len])
        w_b = jnp.concatenate([jnp.where(head0, lt[:c_len], zz),
                               jnp.where(head0, zz, sw[c_len:])], axis=0)
        w_k = jnp.concatenate([jnp.where(head0, sw[:c_len], zz),
                               jnp.where(head0, zz, lt[c_len:])], axis=0)
        return jnp.concatenate([w_b, w_k], axis=1)

    a_all = [_dot(lhs, spread(lt))
             for lhs, lt in zip(stack(at, to_bf16(rt)), lts)]
    l_ab = [jnp.where(strict, a[:c_len, :LANES], 0.0) for a in a_all]
    a_ak = [jnp.where(strict, a[:c_len, LANES:], 0.0).astype(BF16) for a in a_all]
    a_rb = [jnp.where(incl, a[c_len:, :LANES], 0.0).astype(BF16) for a in a_all]
    a_rk = [jnp.where(incl, a[c_len:, LANES:], 0.0).astype(BF16) for a in a_all]

    l_b = to_bf16(l_ab)
    s_acc = [eye + l for l in l_ab]
    q_b = to_bf16(pmm(l_b, l_b))
    for _i in range(4):
        res = pmm(stack(to_bf16(s_acc), q_b), q_b)
        s_acc = [s + r_ for s, r_ in zip(s_acc, top(res))]
        q_b = to_bf16(bot(res))
    s_acc = [s + r_ for s, r_ in zip(s_acc, pmm(to_bf16(s_acc), q_b))]
    t_b = to_bf16(s_acc)

    res = pmm(stack(a_ak, a_rk), vb)
    x2_b, yv2 = to_bf16(top(res)), bot(res)
    def pmm2(xs, ys, zs):
        return [_dot(x, jnp.concatenate([bd(y), bd(z)], axis=1)) for x, y, z in zip(xs, ys, zs)]

    res = pmm2(t_b, at, x2_b)
    a_hat_b = to_bf16([r_[:, :LANES] for r_ in res])
    u_v_b = to_bf16([r_[:, LANES:] for r_ in res])
    res = pmm2(a_rb, a_hat_b, u_v_b)
    r_hat = [r_ + d[:, :LANES] for r_, d in zip(rt, res)]
    y_v = [d[:, LANES:] + y2 for d, y2 in zip(res, yv2)]

    mns = [_dot(lt, jnp.concatenate(
        [jnp.concatenate([ah, jnp.zeros_like(ah)], axis=0),
         jnp.concatenate([uv, v_], axis=0)], axis=1))
        for lt, ah, uv, v_ in zip(lts, a_hat_b, u_v_b, vb)]
    m_p = [jnp.where(head0, mn[:c_len, :LANES], mn[c_len:, :LANES]) + eye for mn in mns]
    n_p = [jnp.where(head0, mn[:c_len, LANES:], mn[c_len:, LANES:]) for mn in mns]

    def decay_tile(pc_row):
        per_row = jnp.broadcast_to(pc_row, (LANES, LANES)).T
        return jnp.where(head0, per_row[:c_len], per_row[c_len:])

    decay = [decay_tile(pc_) for pc_ in pcs]

    lhs = stack(to_bf16(m_p), to_bf16(r_hat))
    states = [h_ref[bb, p] for bb in range(n_rows) for p in range(N_PAIRS)]
    per_chunk = n_rows * N_PAIRS
    ones_bd = ones_ref[...]
    for ci in range(n_chunks):
        sel = slice(ci * per_chunk, (ci + 1) * per_chunk)
        res = pmm(lhs[sel], to_bf16(states))
        states = [(r_[:c_len] + n_) * d_ for r_, n_, d_ in zip(res, n_p[sel], decay[sel])]
        ys = [r_[c_len:] + yv for r_, yv in zip(res, y_v[sel])]
        rows = slice(ci * c_len, (ci + 1) * c_len)
        for bb in range(n_rows):
            y = jnp.concatenate(ys[bb * N_PAIRS:(bb + 1) * N_PAIRS], axis=1)
            mean = _headsum(y, ones_bd) * (1.0 / HEAD_DIM)
            yc = y - mean
            var = _headsum(yc * yc, ones_bd) * (1.0 / HEAD_DIM)
            yn = yc * lax.rsqrt(var + GN_EPS) * lnw_ref[...] + lnb_ref[...]
            o_ref[bb, rows, :] = ((yn + bonus_ref[bb, rows, :]) * gate_ref[bb, rows, :]).astype(BF16)
    for j, h_new in enumerate(states):
        h_ref[j // N_PAIRS, j % N_PAIRS] = h_new


def _rwkv_call(feats, ln_w, ln_b, n_chunks=2):
    at, vb, lt, rt, gate, bonus, pc = feats
    b, s, _ = at.shape
    nb = 2 if b % 2 == 0 else 1
    span = n_chunks * CHUNK
    const = lambda bb, t: (0, 0)
    vec = pl.BlockSpec((1, D_MODEL), const)
    tok = pl.BlockSpec((nb, span, D_MODEL), lambda bb, t: (bb, t, 0))
    return pl.pallas_call(
        _rwkv_kernel,
        grid=(b // nb, s // span),
        in_specs=[tok, tok,
                  pl.BlockSpec((nb, n_chunks, N_PAIRS, LANES, LANES),
                               lambda bb, t: (bb, t, 0, 0, 0)),
                  tok, tok, tok,
                  pl.BlockSpec((nb, n_chunks, 1, D_MODEL), lambda bb, t: (bb, t, 0, 0)),
                  vec, vec, pl.BlockSpec((GROUP, GROUP), const)],
        out_specs=tok,
        out_shape=jax.ShapeDtypeStruct((b, s, D_MODEL), BF16),
        scratch_shapes=[pltpu.VMEM((nb, N_PAIRS, CHUNK, LANES), F32)],
        compiler_params=pltpu.CompilerParams(dimension_semantics=("arbitrary", "arbitrary")),
        name="rwkv7_mixer",
    )(at, vb, lt, rt, gate, bonus, pc, ln_w, ln_b, _head_ones())


def _merge_kernel(x_ref, g_ref, wg_ref, o1_ref, o4_ref, o16_ref, m1_ref, m4_ref, m16_ref,
                  l1_ref, l4_ref, l16_ref, ob_ref, pa_ref, pb_ref, wo_ref, e_ref, out_ref,
                  o_scr, stat_scr):
    x = x_ref[0]
    h = _rmsnorm(x, g_ref[...]).astype(BF16)
    gl = _dot(h, wg_ref[...])
    g_a = jax.nn.sigmoid(gl[:, :D_MODEL])
    g_b = jax.nn.sigmoid(gl[:, D_MODEL:])

    def natural_order(src_ref, dst_ref):
        _, d, rows, width = src_ref.shape
        for r in range(d):
            for cblk in range(width // LANES):
                val = src_ref[0, r, :, cblk * LANES:(cblk + 1) * LANES].astype(F32)
                if d == 1:
                    dst_ref[cblk] = val
                else:
                    dst_ref[cblk, pl.ds(r, rows, stride=d), :] = val

    ms, ls = [], []
    for gi, (m_ref, l_ref) in enumerate(((m1_ref, l1_ref), (m4_ref, l4_ref), (m16_ref, l16_ref))):
        natural_order(m_ref, stat_scr.at[2 * gi])
        natural_order(l_ref, stat_scr.at[2 * gi + 1])
        ms.append(stat_scr[2 * gi, 0])
        ls.append(stat_scr[2 * gi + 1, 0])
    mx = jnp.maximum(jnp.maximum(ms[0], ms[1]), ms[2])
    es = [jnp.exp2(m - mx) for m in ms]
    den = es[0] * ls[0] + es[1] * ls[1] + es[2] * ls[2]
    expand = e_ref[...]
    o_a = jnp.zeros(x.shape, F32)
    for e, o_ref in zip(es, (o1_ref, o4_ref, o16_ref)):
        w = e / den
        hi = w.astype(BF16)
        lo = (w - hi.astype(F32)).astype(BF16)
        natural_order(o_ref, o_scr)
        o_g = jnp.concatenate([o_scr[p] for p in range(N_PAIRS)], axis=1)
        o_a = o_a + _dot(jnp.concatenate([hi, lo], axis=1), expand) * o_g

    merged = (g_a * _dot(o_a.astype(BF16), pa_ref[...])
              + g_b * _dot(ob_ref[0], pb_ref[...]))
    out_ref[0] = x + _dot(merged.astype(BF16), wo_ref[...])


def _merge_call(x, g, wg, os_, ms, ls, ob, pa, pb, wo, tm=256):
    b, s, _ = x.shape
    expand = jnp.kron(jnp.eye(N_HEADS, dtype=F32), jnp.ones((1, HEAD_DIM), F32))
    expand = jnp.concatenate([expand, jnp.zeros((LANES - N_HEADS, D_MODEL), F32)], 0).astype(BF16)
    expand = jnp.concatenate([expand, expand], axis=0)
    const = lambda bb, i: (0, 0)
    wide = pl.BlockSpec((1, tm, D_MODEL), lambda bb, i: (bb, i, 0))
    sq = pl.BlockSpec((D_MODEL, D_MODEL), const)
    res = lambda a: pl.BlockSpec((1, a.shape[1], tm // a.shape[1], a.shape[3]),
                                 lambda bb, i: (bb, 0, i, 0))
    return pl.pallas_call(
        _merge_kernel,
        grid=(b, s // tm),
        in_specs=[wide, pl.BlockSpec((1, D_MODEL), const),
                  pl.BlockSpec((D_MODEL, 2 * D_MODEL), const),
                  *[res(a) for a in (*os_, *ms, *ls)], wide, sq, sq, sq,
                  pl.BlockSpec((2 * LANES, D_MODEL), const)],
        out_specs=wide,
        out_shape=jax.ShapeDtypeStruct((b, s, D_MODEL), F32),
        scratch_shapes=[pltpu.VMEM((N_PAIRS, tm, LANES), F32),
                        pltpu.VMEM((len(ms) + len(ls), 1, tm, LANES), F32)],
        compiler_params=pltpu.CompilerParams(dimension_semantics=("arbitrary", "arbitrary")),
        name="merge_proj",
    )(x, g, wg, *os_, *ms, *ls, ob, pa, pb, wo, expand)


def _ffn_kernel(x_ref, g_ref, wg_ref, wu_ref, wd_ref, gf_ref, out_ref, *, final_norm):
    x = x_ref[...]
    h = _rmsnorm(x, g_ref[...]).astype(BF16)
    act = (jax.nn.silu(_dot(h, wg_ref[...])) * _dot(h, wu_ref[...])).astype(BF16)
    x2 = x + _dot(act, wd_ref[...])
    out_ref[...] = _rmsnorm(x2, gf_ref[...]) if final_norm else x2


def _ffn_call(x2, g, wg, wu, wd, gf, final_norm, tm=512):
    t = x2.shape[0]
    row = lambda i: (i, 0)
    const = lambda i: (0, 0)
    vec = pl.BlockSpec((1, D_MODEL), const)
    resident = lambda shape: pl.BlockSpec(shape, const, pipeline_mode=pl.Buffered(1))
    return pl.pallas_call(
        functools.partial(_ffn_kernel, final_norm=final_norm),
        grid=(t // tm,),
        in_specs=[pl.BlockSpec((tm, D_MODEL), row), vec,
                  resident((D_MODEL, D_FF)), resident((D_MODEL, D_FF)),
                  resident((D_FF, D_MODEL)), vec],
        out_specs=pl.BlockSpec((tm, D_MODEL), row),
        out_shape=jax.ShapeDtypeStruct((t, D_MODEL), F32),
        compiler_params=pltpu.CompilerParams(dimension_semantics=("arbitrary",)),
        name="ffn_final",
    )(x2, g, wg, wu, wd, gf)


def _rotary_tables(seq):
    half = ROPE_DIM // 2
    in_head = np.arange(LANES) % HEAD_DIM
    inv_freq = ROPE_THETA ** (-(in_head % half).astype(np.float64) * (2.0 / ROPE_DIM))
    ang = np.arange(seq, dtype=np.float64)[:, None] * inv_freq[None, :]
    cos, sin = np.cos(ang), np.sin(ang)
    first = (in_head < half)[None, :]
    second = ((in_head >= half) & (in_head < ROPE_DIM))[None, :]
    c = np.where(first | second, cos, 1.0)
    s1 = np.where(second, sin, 0.0)
    s2 = np.where(first, -sin, 0.0)
    return tuple(jnp.asarray(t.astype(np.float32)) for t in (c, s1, s2))


def kernel(x, norm_mix_g, w_in, shift_mu, decay_w0, decay_w2, iclr_a0, iclr_a2, gate_g2, k_k, k_a, r_k, ln_x_w, ln_x_b, proj_attn, proj_rwkv, w_out, norm_ffn_g, ffn_w_gate, ffn_w_up, ffn_w_down, norm_final_g):
    b, s, d = x.shape
    assert d == D_MODEL and s % (max(dil for _, dil in DILATED_GROUPS) * 1024) == 0
    depth = w_in.shape[0]
    c, s1, s2 = _rotary_tables(s)
    vec = lambda a: a.reshape(1, -1).astype(F32)
    n_attn = 3 * D_MODEL
    for l in range(depth):
        g_mix = vec(norm_mix_g[l])
        w = w_in[l]
        qkv = _qkv_call(x, g_mix, w[:, :n_attn].astype(BF16), c, s1, s2)
        zero = jnp.zeros((DECAY_LORA, D_MODEL), F32)
        w2a2 = jnp.concatenate(
            [jnp.concatenate([decay_w2[l], zero], axis=1),
             jnp.concatenate([zero, iclr_a2[l]], axis=1)], axis=0).astype(BF16)
        feats = _rwkv_feat_call(x, g_mix, w[:, n_attn:n_attn + D_SHIFTED].astype(BF16),
                                vec(shift_mu[l]), vec(decay_w0[l]), w2a2, vec(iclr_a0[l]),
                                gate_g2[l].astype(BF16), vec(k_k[l]), vec(k_a[l]), vec(r_k[l]))
        os_, ms, ls = [], [], []
        for gi in range(len(DILATED_GROUPS)):
            o_g, m_g, l_g = _attn_call(*qkv[3 * gi:3 * gi + 3])
            os_.append(o_g)
            ms.append(m_g)
            ls.append(l_g)
        o_b = _rwkv_call(feats, vec(ln_x_w[l]), vec(ln_x_b[l]))
        x = _merge_call(x, g_mix, w[:, n_attn + D_SHIFTED:].astype(BF16), os_, ms, ls, o_b,
                        proj_attn[l].astype(BF16), proj_rwkv[l].astype(BF16),
                        w_out[l].astype(BF16))
        x = _ffn_call(x.reshape(b * s, d), vec(norm_ffn_g[l]), ffn_w_gate[l].astype(BF16),
                      ffn_w_up[l].astype(BF16), ffn_w_down[l].astype(BF16),
                      vec(norm_final_g), final_norm=(l == depth - 1)).reshape(b, s, d)
    return x
```

```python
import functools

import jax
import jax.numpy as jnp
import numpy as np
from jax import lax
from jax.experimental import pallas as pl
from jax.experimental.pallas import tpu as pltpu

F32 = jnp.float32
BF16 = jnp.bfloat16

D_MODEL = 1024
HEAD_DIM = 64
N_HEADS = 16
ROPE_DIM = 16
ROPE_THETA = 500000.0
DILATED_GROUPS = ((128, 1), (512, 4), (2048, 16))
ATTN_BLOCK = 128
DECAY_LORA = 64
ICLR_LORA = 64
GATE_LORA = 128
D_FF = 2816
RMS_EPS = 1e-6
GN_EPS = 64e-5
D_SHIFTED = 3 * D_MODEL + DECAY_LORA + ICLR_LORA + GATE_LORA

LANES = 128
SUBLANES = 8
N_PAIRS = D_MODEL // LANES
CHUNK = 64
GROUP = 256

LOG2_E = 1.4426950408889634
NT_DIMS = (((1,), (1,)), ((), ()))


def _dot(a, b):
    return jnp.dot(a, b, preferred_element_type=F32)


def _dot_nt(a, b):
    return lax.dot_general(a, b, NT_DIMS, preferred_element_type=F32)


def _rmsnorm(x, g):
    return x * lax.rsqrt(jnp.mean(x * x, axis=-1, keepdims=True) + RMS_EPS) * g


N_QKV_OUTS = 3 * len(DILATED_GROUPS)
N_FEAT_OUTS = 7


def _proj_kernel(x_ref, g_ref, wq_ref, c_ref, s1_ref, s2_ref, wr_ref, mu_ref, w0_ref, w2a2_ref,
                 a0_ref, g2_ref, kk_ref, ka_ref, rk_ref, ones_ref, *refs, tm):
    qkv_outs = refs[:N_QKV_OUTS]
    feat_outs = refs[N_QKV_OUTS:N_QKV_OUTS + N_FEAT_OUTS]
    stage_a, stage_b, carry_ref = refs[N_QKV_OUTS + N_FEAT_OUTS:]
    h = _rmsnorm(x_ref[0], g_ref[...]).astype(BF16)
    _qkv_body(h, wq_ref, c_ref, s1_ref, s2_ref, qkv_outs, (stage_a, stage_b), tm)
    _rwkv_feat_body(h, wr_ref, mu_ref, w0_ref, w2a2_ref, a0_ref, g2_ref, kk_ref, ka_ref, rk_ref,
                    ones_ref, *feat_outs, carry_ref, tm=tm)


def _qkv_body(h, w_ref, c_ref, s1_ref, s2_ref, outs, stages, tm):
    dilations = [d for _, d in DILATED_GROUPS]
    assert dilations[0] == 1
    p = _dot(h, w_ref[...])
    c, s1, s2 = c_ref[...], s1_ref[...], s2_ref[...]

    def rot(t):
        return t * c + pltpu.roll(t, 8, 1) * s1 + pltpu.roll(t, LANES - 8, 1) * s2

    for blk in range(N_PAIRS):
        lanes = slice(blk * LANES, (blk + 1) * LANES)
        vals = (rot(p[:, lanes]) * (HEAD_DIM ** -0.5 * LOG2_E),
                rot(p[:, D_MODEL + blk * LANES:D_MODEL + (blk + 1) * LANES]),
                p[:, 2 * D_MODEL + blk * LANES:2 * D_MODEL + (blk + 1) * LANES])
        for j, val in enumerate(vals):
            stages[0][j * N_PAIRS + blk] = val
            outs[j][0, 0, :, lanes] = val.astype(BF16)

    d_prev = 1
    for gi in range(1, len(dilations)):
        d = dilations[gi]
        step = d // d_prev
        assert step * d_prev == d
        rows_prev, rows = tm // d_prev, tm // d
        src_ref, dst_ref = stages[(gi - 1) % 2], stages[gi % 2]
        for j in range(3):
            for blk in range(N_PAIRS):
                slab = j * N_PAIRS + blk
                lanes = slice(blk * LANES, (blk + 1) * LANES)
                for r_prev in range(d_prev):
                    for off in range(step):
                        r = d_prev * off + r_prev
                        val = src_ref[slab, pl.ds(r_prev * rows_prev + off, rows, stride=step), :]
                        outs[3 * gi + j][0, r, :, lanes] = val.astype(BF16)
                        if gi + 1 < len(dilations):
                            dst_ref[slab, r * rows:(r + 1) * rows, :] = val
        d_prev = d


def _rwkv_feat_body(h, w_ref, mu_ref, w0_ref, w2a2_ref, a0_ref, g2_ref, kk_ref, ka_ref, rk_ref,
                    ones_ref, at_ref, vb_ref, lt_ref, rt_ref, gate_ref, bonus_ref, pc_ref,
                    carry_ref, *, tm):
    @pl.when(pl.program_id(1) == 0)
    def _():
        carry_ref[...] = jnp.zeros_like(carry_ref)

    lane = lax.broadcasted_iota(jnp.int32, (1, LANES), 1)
    head0 = lane < HEAD_DIM
    ones_bd = ones_ref[...]

    cols = _dot(h, w_ref[...])
    row8 = lax.broadcasted_iota(jnp.int32, (SUBLANES, 1), 0)
    carry = carry_ref[...]
    carry_ref[...] = cols[tm - 1:tm, :]
    mu = mu_ref[...]
    one_minus_mu = 1.0 - mu

    def shifted(ci, lanes):
        cur = cols[ci * CHUNK:(ci + 1) * CHUNK, lanes]
        last = carry[:, lanes] if ci == 0 else cols[ci * CHUNK - 1:ci * CHUNK, lanes]
        prev = pltpu.roll(cur, 1, 0)
        prev = jnp.concatenate([jnp.where(row8 == 0, last, prev[:SUBLANES]),
                                prev[SUBLANES:]], axis=0)
        return cur * one_minus_mu[:, lanes] + prev * mu[:, lanes]

    def prefix_sum(x):
        outs, total = [], None
        for gi in range(x.shape[0] // SUBLANES):
            xg = x[gi * SUBLANES:(gi + 1) * SUBLANES]
            for sh in (1, 2, 4):
                xg = xg + jnp.where(row8 >= sh, pltpu.roll(xg, sh, 0), 0.0)
            if total is not None:
                xg = xg + total
            total = xg[SUBLANES - 1:SUBLANES]
            outs.append(xg)
        return jnp.concatenate(outs, axis=0)

    w0, a0, k_k, k_a, r_k = (ref[...] for ref in (w0_ref, a0_ref, kk_ref, ka_ref, rk_ref))
    pairs_per_group = GROUP // LANES
    for ci in range(tm // CHUNK):
        rows = slice(ci * CHUNK, (ci + 1) * CHUNK)
        tail = shifted(ci, slice(3 * D_MODEL, D_SHIFTED))
        slab, g_lo = tail[:, :LANES], tail[:, LANES:]
        z = jnp.where(head0, jnp.tanh(slab), slab).astype(BF16)
        lora = _dot(z, w2a2_ref[...])
        gate_ref[0, rows, :] = _dot(jax.nn.sigmoid(g_lo).astype(BF16), g2_ref[...])
        for gi in range(D_MODEL // GROUP):
            lanes = slice(gi * GROUP, (gi + 1) * GROUP)
            at = lambda base: slice(base + gi * GROUP, base + (gi + 1) * GROUP)
            r, k, v = shifted(ci, at(0)), shifted(ci, at(D_MODEL)), shifted(ci, at(2 * D_MODEL))
            lw = jax.nn.sigmoid(w0[:, lanes] + lora[:, lanes]) * (-(jnp.e ** -0.5))
            eta = jax.nn.sigmoid(a0[:, lanes] + lora[:, at(D_MODEL)])
            kk = k * k_k[:, lanes]
            kk = kk * jnp.minimum(lax.rsqrt(_headsum(kk * kk, ones_bd)), 1e12)
            k_mod = k * (1.0 + (eta - 1.0) * k_a[:, lanes])
            bonus_ref[0, rows, lanes] = _headsum(r * k_mod * r_k[:, lanes], ones_bd) * v
            b_s = kk * eta

            cl = prefix_sum(lw)
            e_ncl = jnp.exp(-cl)
            at_ref[0, rows, lanes] = (-kk * jnp.exp(cl - lw)).astype(BF16)
            rt_ref[0, rows, lanes] = r * jnp.exp(cl)
            vb_ref[0, rows, lanes] = v.astype(BF16)
            pc_ref[0, ci, :, lanes] = jnp.exp(cl[CHUNK - 1:CHUNK, :])
            b_t = b_s * e_ncl
            k_t = k_mod * e_ncl
            for pp in range(pairs_per_group):
                sl = slice(pp * LANES, (pp + 1) * LANES)
                lt_ref[0, ci, gi * pairs_per_group + pp] = jnp.concatenate(
                    [b_t[:, sl], k_t[:, sl]], axis=0).T.astype(BF16)


def _headsum(x, ones_bd):
    parts = []
    for gi in range(x.shape[1] // GROUP):
        xg = x[:, gi * GROUP:(gi + 1) * GROUP]
        hi = xg.astype(BF16)
        lo = (xg - hi.astype(F32)).astype(BF16)
        parts.append(_dot(hi, ones_bd) + _dot(lo, ones_bd))
    return jnp.concatenate(parts, axis=1)


def _head_ones():
    return jnp.kron(jnp.eye(GROUP // HEAD_DIM, dtype=F32),
                    jnp.ones((HEAD_DIM, HEAD_DIM), F32)).astype(BF16)


def _proj_call(x, g, wq, c, s1, s2, wr, mu, w0, w2a2, a0, g2, k_k, k_a, r_k, tm=256):
    b, s, _ = x.shape
    n_chunks = tm // CHUNK
    const = lambda bb, i: (0, 0)
    pos = lambda bb, i: (i, 0)
    vec = pl.BlockSpec((1, D_MODEL), const)
    resident = lambda shape: pl.BlockSpec(shape, const, pipeline_mode=pl.Buffered(1))
    tok_spec = pl.BlockSpec((1, tm, D_MODEL), lambda bb, i: (bb, i, 0))
    tok_shape = lambda dt: jax.ShapeDtypeStruct((b, s, D_MODEL), dt)
    out_specs, out_shape = [], []
    for _, d in DILATED_GROUPS:
        for _j in range(3):
            out_specs.append(pl.BlockSpec((1, d, tm // d, D_MODEL), lambda bb, i: (bb, 0, i, 0)))
            out_shape.append(jax.ShapeDtypeStruct((b, d, s // d, D_MODEL), BF16))
    out_specs += [tok_spec, tok_spec,
                  pl.BlockSpec((1, n_chunks, N_PAIRS, LANES, LANES), lambda bb, i: (bb, i, 0, 0, 0)),
                  tok_spec, tok_spec, tok_spec,
                  pl.BlockSpec((1, n_chunks, 1, D_MODEL), lambda bb, i: (bb, i, 0, 0))]
    out_shape += [tok_shape(BF16), tok_shape(BF16),
                  jax.ShapeDtypeStruct((b, s // CHUNK, N_PAIRS, LANES, LANES), BF16),
                  tok_shape(F32), tok_shape(F32), tok_shape(F32),
                  jax.ShapeDtypeStruct((b, s // CHUNK, 1, D_MODEL), F32)]
    outs = pl.pallas_call(
        functools.partial(_proj_kernel, tm=tm),
        grid=(b, s // tm),
        in_specs=[tok_spec, vec,
                  resident((D_MODEL, 3 * D_MODEL)),
                  pl.BlockSpec((tm, LANES), pos), pl.BlockSpec((tm, LANES), pos),
                  pl.BlockSpec((tm, LANES), pos),
                  resident((D_MODEL, D_SHIFTED)),
                  pl.BlockSpec((1, D_SHIFTED), const), vec,
                  resident((LANES, 2 * D_MODEL)), vec,
                  resident((GATE_LORA, D_MODEL)), vec, vec, vec,
                  resident((GROUP, GROUP))],
        out_specs=out_specs,
        out_shape=out_shape,
        scratch_shapes=[pltpu.VMEM((3 * N_PAIRS, tm, LANES), F32),
                        pltpu.VMEM((3 * N_PAIRS, tm, LANES), F32),
                        pltpu.VMEM((1, D_SHIFTED), F32)],
        compiler_params=pltpu.CompilerParams(dimension_semantics=("arbitrary", "arbitrary")),
        name="in_proj",
    )(x, g, wq, c, s1, s2, wr, mu, w0, w2a2, a0, g2, k_k, k_a, r_k, _head_ones())
    return outs[:N_QKV_OUTS], outs[N_QKV_OUTS:]


def _attn_kernel(q_ref, kc_ref, kp_ref, vc_ref, vp_ref, o_ref, m_ref, l_ref, *, tq):
    n = pl.program_id(2)
    blk = ATTN_BLOCK
    qi = lax.broadcasted_iota(jnp.int32, (blk, 1), 0)
    kc = lax.broadcasted_iota(jnp.int32, (1, blk), 1)
    upper = kc > qi
    diag_f = (kc == qi).astype(F32)
    upper_b = upper.astype(BF16)
    lower_b = (kc <= qi).astype(BF16)
    diag_b = diag_f.astype(BF16)
    lane = lax.broadcasted_iota(jnp.int32, (1, LANES), 1)
    head0 = lane < HEAD_DIM
    ones_v = jnp.ones((2 * blk, LANES), BF16)

    for i in range(tq // blk):
        r0 = i * blk
        m_tile = jnp.zeros((blk, LANES), F32)
        l_tile = jnp.ones((blk, LANES), F32)
        for hp in range(N_PAIRS):
            l0 = hp * LANES
            qs = q_ref[r0:r0 + blk, l0:l0 + LANES]
            if i == 0:
                kprev = kp_ref[:, l0:l0 + LANES]
                vprev = vp_ref[:, l0:l0 + LANES]
            else:
                kprev = kc_ref[r0 - blk:r0, l0:l0 + LANES]
                vprev = vc_ref[r0 - blk:r0, l0:l0 + LANES]
            k2 = jnp.concatenate([kprev, kc_ref[r0:r0 + blk, l0:l0 + LANES]], axis=0)
            v2 = jnp.concatenate([vprev, vc_ref[r0:r0 + blk, l0:l0 + LANES]], axis=0)
            zq = jnp.zeros_like(qs)
            q_st = jnp.concatenate([jnp.where(head0, qs, zq), jnp.where(head0, zq, qs)], axis=0)
            s = _dot_nt(q_st, k2)
            ps, ms = [], []
            for hh in range(2):
                s_prev = s[hh * blk:(hh + 1) * blk, :blk]
                s_cur = s[hh * blk:(hh + 1) * blk, blk:]
                if i == 0:
                    s_prev = jnp.where(n > 0, s_prev, -1e30)
                s_far = jnp.sum(s_prev * diag_f, axis=1, keepdims=True)
                s_tile = jnp.where(upper, s_prev, s_cur)
                m = jnp.maximum(jnp.max(s_tile, axis=1, keepdims=True), s_far)
                p = jnp.exp2(s_tile - m).astype(BF16)
                p_far = jnp.exp2(s_far - m).astype(BF16)
                ps.append(jnp.concatenate([p * upper_b + p_far * diag_b, p * lower_b], axis=1))
                ms.append(m)
            o2 = _dot(jnp.concatenate(ps, axis=0),
                      jnp.concatenate([v2, ones_v], axis=1))
            o = jnp.where(head0, o2[:blk, :LANES], o2[blk:, :LANES])
            o_ref[r0:r0 + blk, l0:l0 + LANES] = o.astype(BF16)
            for hh in range(2):
                mine = lane == 2 * hp + hh
                m_tile = jnp.where(mine, ms[hh], m_tile)
                l_tile = jnp.where(mine, o2[hh * blk:(hh + 1) * blk, LANES:], l_tile)
        m_ref[r0:r0 + blk, :] = m_tile
        l_ref[r0:r0 + blk, :] = l_tile


def _attn_call(q, k, v, tq=1024):
    b, d, sub, _ = q.shape
    cur = lambda bb, r, n: (bb, r, n, 0)
    prev = lambda bb, r, n: (bb, r, jnp.maximum(n * (tq // ATTN_BLOCK) - 1, 0), 0)
    stat = jax.ShapeDtypeStruct((b, d, sub, LANES), F32)
    return pl.pallas_call(
        functools.partial(_attn_kernel, tq=tq),
        grid=(b, d, sub // tq),
        in_specs=[pl.BlockSpec((None, None, tq, D_MODEL), cur),
                  pl.BlockSpec((None, None, tq, D_MODEL), cur),
                  pl.BlockSpec((None, None, ATTN_BLOCK, D_MODEL), prev),
                  pl.BlockSpec((None, None, tq, D_MODEL), cur),
                  pl.BlockSpec((None, None, ATTN_BLOCK, D_MODEL), prev)],
        out_specs=[pl.BlockSpec((None, None, tq, D_MODEL), cur),
                   pl.BlockSpec((None, None, tq, LANES), cur),
                   pl.BlockSpec((None, None, tq, LANES), cur)],
        out_shape=[jax.ShapeDtypeStruct((b, d, sub, D_MODEL), BF16), stat, stat],
        compiler_params=pltpu.CompilerParams(
            dimension_semantics=("arbitrary", "arbitrary", "arbitrary")),
        name=f"dilated_attn_d{d}",
    )(q, k, k, v, v)


def _rwkv_kernel(at_ref, vb_ref, lt_ref, rt_ref, gate_ref, bonus_ref, pc_ref,
                 lnw_ref, lnb_ref, ones_ref, o_ref, h_ref):
    c_len = CHUNK

    @pl.when(pl.program_id(1) == 0)
    def _():
        h_ref[...] = jnp.zeros_like(h_ref)

    row = lax.broadcasted_iota(jnp.int32, (c_len, 1), 0)
    lane = lax.broadcasted_iota(jnp.int32, (1, LANES), 1)
    head0 = lane < HEAD_DIM

    col = lane % HEAD_DIM
    strict = row > col
    incl = row >= col
    eye = (row == col).astype(F32)
    n_rows = at_ref.shape[0]
    n_chunks = at_ref.shape[1] // c_len
    chains = [(ci, bb, p) for ci in range(n_chunks) for bb in range(n_rows)
              for p in range(N_PAIRS)]
    cut = lambda ref: [ref[bb, ci * c_len:(ci + 1) * c_len, p * LANES:(p + 1) * LANES]
                       for ci, bb, p in chains]
    at, rt, vb = (cut(ref) for ref in (at_ref, rt_ref, vb_ref))
    pcs = [pc_ref[bb, ci, :, p * LANES:(p + 1) * LANES] for ci, bb, p in chains]
    lts = [lt_ref[bb, ci, p] for ci, bb, p in chains]
    top = lambda ts: [t[:c_len] for t in ts]
    bot = lambda ts: [t[c_len:] for t in ts]
    stack = lambda xs, ys: [jnp.concatenate([x, y], axis=0) for x, y in zip(xs, ys)]
    to_bf16 = lambda ts: [t.astype(BF16) for t in ts]

    def bd(y):
        zz = jnp.zeros_like(y)
        return jnp.concatenate([jnp.where(head0, y, zz), jnp.where(head0, zz, y)], axis=0)

    def pmm(xs, ys):
        return [_dot(x, bd(y)) for x, y in zip(xs, ys)]

    def spread(lt):
        sw = pltpu.roll(lt, HEAD_DIM, 1)
        zz = jnp.zeros_like(lt[:c_len])
        w_b = jnp.concatenate([jnp.where(head0, lt[:c_len], zz),
                               jnp.where(head0, zz, sw[c_len:])], axis=0)
        w_k = jnp.concatenate([jnp.where(head0, sw[:c_len], zz),
                               jnp.where(head0, zz, lt[c_len:])], axis=0)
        return jnp.concatenate([w_b, w_k], axis=1)

    a_all = [_dot(lhs, spread(lt))
             for lhs, lt in zip(stack(at, to_bf16(rt)), lts)]
    l_ab = [jnp.where(strict, a[:c_len, :LANES], 0.0) for a in a_all]
    a_ak = [jnp.where(strict, a[:c_len, LANES:], 0.0).astype(BF16) for a in a_all]
    a_rb = [jnp.where(incl, a[c_len:, :LANES], 0.0).astype(BF16) for a in a_all]
    a_rk = [jnp.where(incl, a[c_len:, LANES:], 0.0).astype(BF16) for a in a_all]

    l_b = to_bf16(l_ab)
    s_acc = [eye + l for l in l_ab]
    q_b = to_bf16(pmm(l_b, l_b))
    for _i in range(4):
        res = pmm(stack(to_bf16(s_acc), q_b), q_b)
        s_acc = [s + r_ for s, r_ in zip(s_acc, top(res))]
        q_b = to_bf16(bot(res))
    s_acc = [s + r_ for s, r_ in zip(s_acc, pmm(to_bf16(s_acc), q_b))]
    t_b = to_bf16(s_acc)

    res = pmm(stack(a_ak, a_rk), vb)
    x2_b, yv2 = to_bf16(top(res)), bot(res)
    def pmm2(xs, ys, zs):
        return [_dot(x, jnp.concatenate([bd(y), bd(z)], axis=1)) for x, y, z in zip(xs, ys, zs)]

    res = pmm2(t_b, at, x2_b)
    a_hat_b = to_bf16([r_[:, :LANES] for r_ in res])
    u_v_b = to_bf16([r_[:, LANES:] for r_ in res])
    res = pmm2(a_rb, a_hat_b, u_v_b)
    r_hat = [r_ + d[:, :LANES] for r_, d in zip(rt, res)]
    y_v = [d[:, LANES:] + y2 for d, y2 in zip(res, yv2)]

    mns = [_dot(lt, jnp.concatenate(
        [jnp.concatenate([ah, jnp.zeros_like(ah)], axis=0),
         jnp.concatenate([uv, v_], axis=0)], axis=1))
        for lt, ah, uv, v_ in zip(lts, a_hat_b, u_v_b, vb)]
    m_p = [jnp.where(head0, mn[:c_len, :LANES], mn[c_len:, :LANES]) + eye for mn in mns]
    n_p = [jnp.where(head0, mn[:c_len, LANES:], mn[c_len:, LANES:]) for mn in mns]

    def decay_tile(pc_row):
        per_row = jnp.broadcast_to(pc_row, (LANES, LANES)).T
        return jnp.where(head0, per_row[:c_len], per_row[c_len:])

    decay = [decay_tile(pc_) for pc_ in pcs]

    lhs = stack(to_bf16(m_p), to_bf16(r_hat))
    states = [h_ref[bb, p] for bb in range(n_rows) for p in range(N_PAIRS)]
    per_chunk = n_rows * N_PAIRS
    ones_bd = ones_ref[...]
    for ci in range(n_chunks):
        sel = slice(ci * per_chunk, (ci + 1) * per_chunk)
        res = pmm(lhs[sel], to_bf16(states))
        states = [(r_[:c_len] + n_) * d_ for r_, n_, d_ in zip(res, n_p[sel], decay[sel])]
        ys = [r_[c_len:] + yv for r_, yv in zip(res, y_v[sel])]
        rows = slice(ci * c_len, (ci + 1) * c_len)
        for bb in range(n_rows):
            y = jnp.concatenate(ys[bb * N_PAIRS:(bb + 1) * N_PAIRS], axis=1)
            mean = _headsum(y, ones_bd) * (1.0 / HEAD_DIM)
            yc = y - mean
            var = _headsum(yc * yc, ones_bd) * (1.0 / HEAD_DIM)
            yn = yc * lax.rsqrt(var + GN_EPS) * lnw_ref[...] + lnb_ref[...]
            o_ref[bb, rows, :] = ((yn + bonus_ref[bb, rows, :]) * gate_ref[bb, rows, :]).astype(BF16)
    for j, h_new in enumerate(states):
        h_ref[j // N_PAIRS, j % N_PAIRS] = h_new


def _rwkv_call(feats, ln_w, ln_b, n_chunks=2):
    at, vb, lt, rt, gate, bonus, pc = feats
    b, s, _ = at.shape
    nb = 2 if b % 2 == 0 else 1
    span = n_chunks * CHUNK
    const = lambda bb, t: (0, 0)
    vec = pl.BlockSpec((1, D_MODEL), const)
    tok = pl.BlockSpec((nb, span, D_MODEL), lambda bb, t: (bb, t, 0))
    return pl.pallas_call(
        _rwkv_kernel,
        grid=(b // nb, s // span),
        in_specs=[tok, tok,
                  pl.BlockSpec((nb, n_chunks, N_PAIRS, LANES, LANES),
                               lambda bb, t: (bb, t, 0, 0, 0)),
                  tok, tok, tok,
                  pl.BlockSpec((nb, n_chunks, 1, D_MODEL), lambda bb, t: (bb, t, 0, 0)),
                  vec, vec, pl.BlockSpec((GROUP, GROUP), const)],
        out_specs=tok,
        out_shape=jax.ShapeDtypeStruct((b, s, D_MODEL), BF16),
        scratch_shapes=[pltpu.VMEM((nb, N_PAIRS, CHUNK, LANES), F32)],
        compiler_params=pltpu.CompilerParams(dimension_semantics=("arbitrary", "arbitrary")),
        name="rwkv7_mixer",
    )(at, vb, lt, rt, gate, bonus, pc, ln_w, ln_b, _head_ones())


def _merge_kernel(x_ref, g_ref, wg_ref, o1_ref, o4_ref, o16_ref, m1_ref, m4_ref, m16_ref,
                  l1_ref, l4_ref, l16_ref, ob_ref, pa_ref, pb_ref, wo_ref, e_ref, out_ref,
                  o_scr, stat_scr):
    x = x_ref[0]
    h = _rmsnorm(x, g_ref[...]).astype(BF16)
    gl = _dot(h, wg_ref[...])
    g_a = jax.nn.sigmoid(gl[:, :D_MODEL])
    g_b = jax.nn.sigmoid(gl[:, D_MODEL:])

    def natural_order(src_ref, dst_ref):
        _, d, rows, width = src_ref.shape
        for r in range(d):
            for cblk in range(width // LANES):
                val = src_ref[0, r, :, cblk * LANES:(cblk + 1) * LANES].astype(F32)
                if d == 1:
                    dst_ref[cblk] = val
                else:
                    dst_ref[cblk, pl.ds(r, rows, stride=d), :] = val

    ms, ls = [], []
    for gi, (m_ref, l_ref) in enumerate(((m1_ref, l1_ref), (m4_ref, l4_ref), (m16_ref, l16_ref))):
        natural_order(m_ref, stat_scr.at[2 * gi])
        natural_order(l_ref, stat_scr.at[2 * gi + 1])
        ms.append(stat_scr[2 * gi, 0])
        ls.append(stat_scr[2 * gi + 1, 0])
    mx = jnp.maximum(jnp.maximum(ms[0], ms[1]), ms[2])
    es = [jnp.exp2(m - mx) for m in ms]
    den = es[0] * ls[0] + es[1] * ls[1] + es[2] * ls[2]
    expand = e_ref[...]
    o_a = jnp.zeros(x.shape, F32)
    for e, o_ref in zip(es, (o1_ref, o4_ref, o16_ref)):
        w = e / den
        hi = w.astype(BF16)
        lo = (w - hi.astype(F32)).astype(BF16)
        natural_order(o_ref, o_scr)
        o_g = jnp.concatenate([o_scr[p] for p in range(N_PAIRS)], axis=1)
        o_a = o_a + _dot(jnp.concatenate([hi, lo], axis=1), expand) * o_g

    merged = (g_a * _dot(o_a.astype(BF16), pa_ref[...])
              + g_b * _dot(ob_ref[0], pb_ref[...]))
    out_ref[0] = x + _dot(merged.astype(BF16), wo_ref[...])


def _merge_call(x, g, wg, os_, ms, ls, ob, pa, pb, wo, tm=256):
    b, s, _ = x.shape
    expand = jnp.kron(jnp.eye(N_HEADS, dtype=F32), jnp.ones((1, HEAD_DIM), F32))
    expand = jnp.concatenate([expand, jnp.zeros((LANES - N_HEADS, D_MODEL), F32)], 0).astype(BF16)
    expand = jnp.concatenate([expand, expand], axis=0)
    const = lambda bb, i: (0, 0)
    wide = pl.BlockSpec((1, tm, D_MODEL), lambda bb, i: (bb, i, 0))
    sq = pl.BlockSpec((D_MODEL, D_MODEL), const)
    res = lambda a: pl.BlockSpec((1, a.shape[1], tm // a.shape[1], a.shape[3]),
                                 lambda bb, i: (bb, 0, i, 0))
    return pl.pallas_call(
        _merge_kernel,
        grid=(b, s // tm),
        in_specs=[wide, pl.BlockSpec((1, D_MODEL), const),
                  pl.BlockSpec((D_MODEL, 2 * D_MODEL), const),
                  *[res(a) for a in (*os_, *ms, *ls)], wide, sq, sq, sq,
                  pl.BlockSpec((2 * LANES, D_MODEL), const)],
        out_specs=wide,
        out_shape=jax.ShapeDtypeStruct((b, s, D_MODEL), F32),
        scratch_shapes=[pltpu.VMEM((N_PAIRS, tm, LANES), F32),
                        pltpu.VMEM((len(ms) + len(ls), 1, tm, LANES), F32)],
        compiler_params=pltpu.CompilerParams(dimension_semantics=("arbitrary", "arbitrary")),
        name="merge_proj",
    )(x, g, wg, *os_, *ms, *ls, ob, pa, pb, wo, expand)


def _ffn_kernel(x_ref, g_ref, wg_ref, wu_ref, wd_ref, gf_ref, out_ref, *, final_norm):
    x = x_ref[...]
    h = _rmsnorm(x, g_ref[...]).astype(BF16)
    act = (jax.nn.silu(_dot(h, wg_ref[...])) * _dot(h, wu_ref[...])).astype(BF16)
    x2 = x + _dot(act, wd_ref[...])
    out_ref[...] = _rmsnorm(x2, gf_ref[...]) if final_norm else x2


def _ffn_call(x2, g, wg, wu, wd, gf, final_norm, tm=512):
    t = x2.shape[0]
    row = lambda i: (i, 0)
    const = lambda i: (0, 0)
    vec = pl.BlockSpec((1, D_MODEL), const)
    resident = lambda shape: pl.BlockSpec(shape, const, pipeline_mode=pl.Buffered(1))
    return pl.pallas_call(
        functools.partial(_ffn_kernel, final_norm=final_norm),
        grid=(t // tm,),
        in_specs=[pl.BlockSpec((tm, D_MODEL), row), vec,
                  resident((D_MODEL, D_FF)), resident((D_MODEL, D_FF)),
                  resident((D_FF, D_MODEL)), vec],
        out_specs=pl.BlockSpec((tm, D_MODEL), row),
        out_shape=jax.ShapeDtypeStruct((t, D_MODEL), F32),
        compiler_params=pltpu.CompilerParams(dimension_semantics=("arbitrary",)),
        name="ffn_final",
    )(x2, g, wg, wu, wd, gf)


def _rotary_tables(seq):
    half = ROPE_DIM // 2
    in_head = np.arange(LANES) % HEAD_DIM
    inv_freq = ROPE_THETA ** (-(in_head % half).astype(np.float64) * (2.0 / ROPE_DIM))
    ang = np.arange(seq, dtype=np.float64)[:, None] * inv_freq[None, :]
    cos, sin = np.cos(ang), np.sin(ang)
    first = (in_head < half)[None, :]
    second = ((in_head >= half) & (in_head < ROPE_DIM))[None, :]
    c = np.where(first | second, cos, 1.0)
    s1 = np.where(second, sin, 0.0)
    s2 = np.where(first, -sin, 0.0)
    return tuple(jnp.asarray(t.astype(np.float32)) for t in (c, s1, s2))


def kernel(x, norm_mix_g, w_in, shift_mu, decay_w0, decay_w2, iclr_a0, iclr_a2, gate_g2, k_k, k_a, r_k, ln_x_w, ln_x_b, proj_attn, proj_rwkv, w_out, norm_ffn_g, ffn_w_gate, ffn_w_up, ffn_w_down, norm_final_g):
    b, s, d = x.shape
    assert d == D_MODEL and s % (max(dil for _, dil in DILATED_GROUPS) * 1024) == 0
    depth = w_in.shape[0]
    c, s1, s2 = _rotary_tables(s)
    vec = lambda a: a.reshape(1, -1).astype(F32)
    n_attn = 3 * D_MODEL
    for l in range(depth):
        g_mix = vec(norm_mix_g[l])
        w = w_in[l]
        zero = jnp.zeros((DECAY_LORA, D_MODEL), F32)
        w2a2 = jnp.concatenate(
            [jnp.concatenate([decay_w2[l], zero], axis=1),
             jnp.concatenate([zero, iclr_a2[l]], axis=1)], axis=0).astype(BF16)
        qkv, feats = _proj_call(x, g_mix, w[:, :n_attn].astype(BF16), c, s1, s2,
                                w[:, n_attn:n_attn + D_SHIFTED].astype(BF16),
                                vec(shift_mu[l]), vec(decay_w0[l]), w2a2, vec(iclr_a0[l]),
                                gate_g2[l].astype(BF16), vec(k_k[l]), vec(k_a[l]), vec(r_k[l]))
        os_, ms, ls = [], [], []
        for gi in range(len(DILATED_GROUPS)):
            o_g, m_g, l_g = _attn_call(*qkv[3 * gi:3 * gi + 3])
            os_.append(o_g)
            ms.append(m_g)
            ls.append(l_g)
        o_b = _rwkv_call(feats, vec(ln_x_w[l]), vec(ln_x_b[l]))
        x = _merge_call(x, g_mix, w[:, n_attn + D_SHIFTED:].astype(BF16), os_, ms, ls, o_b,
                        proj_attn[l].astype(BF16), proj_rwkv[l].astype(BF16),
                        w_out[l].astype(BF16))
        x = _ffn_call(x.reshape(b * s, d), vec(norm_ffn_g[l]), ffn_w_gate[l].astype(BF16),
                      ffn_w_up[l].astype(BF16), ffn_w_down[l].astype(BF16),
                      vec(norm_final_g), final_norm=(l == depth - 1)).reshape(b, s, d)
    return x
```

```python
import functools

import jax
import jax.numpy as jnp
import numpy as np
from jax import lax
from jax.experimental import pallas as pl
from jax.experimental.pallas import tpu as pltpu

F32 = jnp.float32
BF16 = jnp.bfloat16

D_MODEL = 1024
HEAD_DIM = 64
N_HEADS = 16
ROPE_DIM = 16
ROPE_THETA = 500000.0
DILATED_GROUPS = ((128, 1), (512, 4), (2048, 16))
ATTN_BLOCK = 128
DECAY_LORA = 64
ICLR_LORA = 64
GATE_LORA = 128
D_FF = 2816
RMS_EPS = 1e-6
GN_EPS = 64e-5
D_SHIFTED = 3 * D_MODEL + DECAY_LORA + ICLR_LORA + GATE_LORA

LANES = 128
SUBLANES = 8
N_PAIRS = D_MODEL // LANES
CHUNK = 64
GROUP = 256

LOG2_E = 1.4426950408889634
NT_DIMS = (((1,), (1,)), ((), ()))


def _dot(a, b):
    return jnp.dot(a, b, preferred_element_type=F32)


def _dot_nt(a, b):
    return lax.dot_general(a, b, NT_DIMS, preferred_element_type=F32)


def _rmsnorm(x, g):
    return x * lax.rsqrt(jnp.mean(x * x, axis=-1, keepdims=True) + RMS_EPS) * g


def _qkv_kernel(x_ref, g_ref, w_ref, c_ref, s1_ref, s2_ref, *refs, tm):
    outs, stages = refs[:-2], refs[-2:]
    dilations = [d for _, d in DILATED_GROUPS]
    assert dilations[0] == 1
    h = _rmsnorm(x_ref[0], g_ref[...]).astype(BF16)
    p = _dot(h, w_ref[...])
    c, s1, s2 = c_ref[...], s1_ref[...], s2_ref[...]

    def rot(t):
        return t * c + pltpu.roll(t, 8, 1) * s1 + pltpu.roll(t, LANES - 8, 1) * s2

    for blk in range(N_PAIRS):
        lanes = slice(blk * LANES, (blk + 1) * LANES)
        vals = (rot(p[:, lanes]) * (HEAD_DIM ** -0.5 * LOG2_E),
                rot(p[:, D_MODEL + blk * LANES:D_MODEL + (blk + 1) * LANES]),
                p[:, 2 * D_MODEL + blk * LANES:2 * D_MODEL + (blk + 1) * LANES])
        for j, val in enumerate(vals):
            stages[0][j * N_PAIRS + blk] = val
            outs[j][0, 0, :, lanes] = val.astype(BF16)

    d_prev = 1
    for gi in range(1, len(dilations)):
        d = dilations[gi]
        step = d // d_prev
        assert step * d_prev == d
        rows_prev, rows = tm // d_prev, tm // d
        src_ref, dst_ref = stages[(gi - 1) % 2], stages[gi % 2]
        for j in range(3):
            for blk in range(N_PAIRS):
                slab = j * N_PAIRS + blk
                lanes = slice(blk * LANES, (blk + 1) * LANES)
                for r_prev in range(d_prev):
                    for off in range(step):
                        r = d_prev * off + r_prev
                        val = src_ref[slab, pl.ds(r_prev * rows_prev + off, rows, stride=step), :]
                        outs[3 * gi + j][0, r, :, lanes] = val.astype(BF16)
                        if gi + 1 < len(dilations):
                            dst_ref[slab, r * rows:(r + 1) * rows, :] = val
        d_prev = d


def _qkv_call(x, g, w, c, s1, s2, tm=512):
    b, s, _ = x.shape
    const = lambda bb, i: (0, 0)
    pos = lambda bb, i: (i, 0)
    out_specs, out_shape = [], []
    for _, d in DILATED_GROUPS:
        for _j in range(3):
            out_specs.append(pl.BlockSpec((1, d, tm // d, D_MODEL), lambda bb, i: (bb, 0, i, 0)))
            out_shape.append(jax.ShapeDtypeStruct((b, d, s // d, D_MODEL), BF16))
    return pl.pallas_call(
        functools.partial(_qkv_kernel, tm=tm),
        grid=(b, s // tm),
        in_specs=[pl.BlockSpec((1, tm, D_MODEL), lambda bb, i: (bb, i, 0)),
                  pl.BlockSpec((1, D_MODEL), const),
                  pl.BlockSpec((D_MODEL, 3 * D_MODEL), const),
                  pl.BlockSpec((tm, LANES), pos), pl.BlockSpec((tm, LANES), pos),
                  pl.BlockSpec((tm, LANES), pos)],
        out_specs=out_specs,
        out_shape=out_shape,
        scratch_shapes=[pltpu.VMEM((3 * N_PAIRS, tm, LANES), F32),
                        pltpu.VMEM((3 * N_PAIRS, tm, LANES), F32)],
        compiler_params=pltpu.CompilerParams(dimension_semantics=("arbitrary", "arbitrary")),
        name="qkv_proj",
    )(x, g, w, c, s1, s2)


def _rwkv_feat_kernel(x_ref, g_ref, w_ref, mu_ref, w0_ref, w2a2_ref, a0_ref, g2_ref, kk_ref,
                      ka_ref, rk_ref, ones_ref,
                      at_ref, vb_ref, lt_ref, rt_ref, gate_ref, bonus_ref, pc_ref,
                      carry_ref, *, tm):
    @pl.when(pl.program_id(1) == 0)
    def _():
        carry_ref[...] = jnp.zeros_like(carry_ref)

    lane = lax.broadcasted_iota(jnp.int32, (1, LANES), 1)
    head0 = lane < HEAD_DIM
    ones_bd = ones_ref[...]

    h = _rmsnorm(x_ref[0], g_ref[...]).astype(BF16)
    cols = _dot(h, w_ref[...])
    row8 = lax.broadcasted_iota(jnp.int32, (SUBLANES, 1), 0)
    carry = carry_ref[...]
    carry_ref[...] = cols[tm - 1:tm, :]
    mu = mu_ref[...]
    one_minus_mu = 1.0 - mu

    def shifted(ci, lanes):
        cur = cols[ci * CHUNK:(ci + 1) * CHUNK, lanes]
        last = carry[:, lanes] if ci == 0 else cols[ci * CHUNK - 1:ci * CHUNK, lanes]
        prev = pltpu.roll(cur, 1, 0)
        prev = jnp.concatenate([jnp.where(row8 == 0, last, prev[:SUBLANES]),
                                prev[SUBLANES:]], axis=0)
        return cur * one_minus_mu[:, lanes] + prev * mu[:, lanes]

    def prefix_sum(x):
        outs, total = [], None
        for gi in range(x.shape[0] // SUBLANES):
            xg = x[gi * SUBLANES:(gi + 1) * SUBLANES]
            for sh in (1, 2, 4):
                xg = xg + jnp.where(row8 >= sh, pltpu.roll(xg, sh, 0), 0.0)
            if total is not None:
                xg = xg + total
            total = xg[SUBLANES - 1:SUBLANES]
            outs.append(xg)
        return jnp.concatenate(outs, axis=0)

    w0, a0, k_k, k_a, r_k = (ref[...] for ref in (w0_ref, a0_ref, kk_ref, ka_ref, rk_ref))
    pairs_per_group = GROUP // LANES
    for ci in range(tm // CHUNK):
        rows = slice(ci * CHUNK, (ci + 1) * CHUNK)
        tail = shifted(ci, slice(3 * D_MODEL, D_SHIFTED))
        slab, g_lo = tail[:, :LANES], tail[:, LANES:]
        z = jnp.where(head0, jnp.tanh(slab), slab).astype(BF16)
        lora = _dot(z, w2a2_ref[...])
        gate_ref[0, rows, :] = _dot(jax.nn.sigmoid(g_lo).astype(BF16), g2_ref[...])
        for gi in range(D_MODEL // GROUP):
            lanes = slice(gi * GROUP, (gi + 1) * GROUP)
            at = lambda base: slice(base + gi * GROUP, base + (gi + 1) * GROUP)
            r, k, v = shifted(ci, at(0)), shifted(ci, at(D_MODEL)), shifted(ci, at(2 * D_MODEL))
            lw = jax.nn.sigmoid(w0[:, lanes] + lora[:, lanes]) * (-(jnp.e ** -0.5))
            eta = jax.nn.sigmoid(a0[:, lanes] + lora[:, at(D_MODEL)])
            kk = k * k_k[:, lanes]
            kk = kk * jnp.minimum(lax.rsqrt(_headsum(kk * kk, ones_bd)), 1e12)
            k_mod = k * (1.0 + (eta - 1.0) * k_a[:, lanes])
            bonus_ref[0, rows, lanes] = _headsum(r * k_mod * r_k[:, lanes], ones_bd) * v
            b_s = kk * eta

            cl = prefix_sum(lw)
            e_ncl = jnp.exp(-cl)
            at_ref[0, rows, lanes] = (-kk * jnp.exp(cl - lw)).astype(BF16)
            rt_ref[0, rows, lanes] = r * jnp.exp(cl)
            vb_ref[0, rows, lanes] = v.astype(BF16)
            pc_ref[0, ci, :, lanes] = jnp.exp(cl[CHUNK - 1:CHUNK, :])
            b_t = b_s * e_ncl
            k_t = k_mod * e_ncl
            for pp in range(pairs_per_group):
                sl = slice(pp * LANES, (pp + 1) * LANES)
                lt_ref[0, ci, gi * pairs_per_group + pp] = jnp.concatenate(
                    [b_t[:, sl], k_t[:, sl]], axis=0).T.astype(BF16)


def _headsum(x, ones_bd):
    parts = []
    for gi in range(x.shape[1] // GROUP):
        xg = x[:, gi * GROUP:(gi + 1) * GROUP]
        hi = xg.astype(BF16)
        lo = (xg - hi.astype(F32)).astype(BF16)
        parts.append(_dot(hi, ones_bd) + _dot(lo, ones_bd))
    return jnp.concatenate(parts, axis=1)


def _head_ones():
    return jnp.kron(jnp.eye(GROUP // HEAD_DIM, dtype=F32),
                    jnp.ones((HEAD_DIM, HEAD_DIM), F32)).astype(BF16)


def _rwkv_feat_call(x, g, w, mu, w0, w2a2, a0, g2, k_k, k_a, r_k, tm=256):
    b, s, _ = x.shape
    n_chunks = tm // CHUNK
    const = lambda bb, i: (0, 0)
    vec = pl.BlockSpec((1, D_MODEL), const)
    tok = lambda bb, i: (bb, i, 0)
    tok_spec = pl.BlockSpec((1, tm, D_MODEL), tok)
    tok_shape = lambda dt: jax.ShapeDtypeStruct((b, s, D_MODEL), dt)
    return pl.pallas_call(
        functools.partial(_rwkv_feat_kernel, tm=tm),
        grid=(b, s // tm),
        in_specs=[tok_spec, vec,
                  pl.BlockSpec((D_MODEL, D_SHIFTED), const),
                  pl.BlockSpec((1, D_SHIFTED), const), vec,
                  pl.BlockSpec((LANES, 2 * D_MODEL), const), vec,
                  pl.BlockSpec((GATE_LORA, D_MODEL), const), vec, vec, vec,
                  pl.BlockSpec((GROUP, GROUP), const)],
        out_specs=[tok_spec, tok_spec,
                   pl.BlockSpec((1, n_chunks, N_PAIRS, LANES, LANES),
                                lambda bb, i: (bb, i, 0, 0, 0)),
                   tok_spec, tok_spec, tok_spec,
                   pl.BlockSpec((1, n_chunks, 1, D_MODEL), lambda bb, i: (bb, i, 0, 0))],
        out_shape=[tok_shape(BF16), tok_shape(BF16),
                   jax.ShapeDtypeStruct((b, s // CHUNK, N_PAIRS, LANES, LANES), BF16),
                   tok_shape(F32), tok_shape(F32), tok_shape(F32),
                   jax.ShapeDtypeStruct((b, s // CHUNK, 1, D_MODEL), F32)],
        scratch_shapes=[pltpu.VMEM((1, D_SHIFTED), F32)],
        compiler_params=pltpu.CompilerParams(dimension_semantics=("arbitrary", "arbitrary")),
        name="rwkv_feat",
    )(x, g, w, mu, w0, w2a2, a0, g2, k_k, k_a, r_k, _head_ones())


def _attn_kernel(q_ref, kc_ref, kp_ref, vc_ref, vp_ref, o_ref, m_ref, l_ref, *, tq):
    n = pl.program_id(2)
    blk = ATTN_BLOCK
    qi = lax.broadcasted_iota(jnp.int32, (blk, 1), 0)
    kc = lax.broadcasted_iota(jnp.int32, (1, blk), 1)
    upper = kc > qi
    diag_f = (kc == qi).astype(F32)
    upper_b = upper.astype(BF16)
    lower_b = (kc <= qi).astype(BF16)
    diag_b = diag_f.astype(BF16)
    lane = lax.broadcasted_iota(jnp.int32, (1, LANES), 1)
    head0 = lane < HEAD_DIM
    ones_v = jnp.ones((2 * blk, LANES), BF16)

    for i in range(tq // blk):
        r0 = i * blk
        m_tile = jnp.zeros((blk, LANES), F32)
        l_tile = jnp.ones((blk, LANES), F32)
        for hp in range(N_PAIRS):
            l0 = hp * LANES
            qs = q_ref[r0:r0 + blk, l0:l0 + LANES]
            if i == 0:
                kprev = kp_ref[:, l0:l0 + LANES]
                vprev = vp_ref[:, l0:l0 + LANES]
            else:
                kprev = kc_ref[r0 - blk:r0, l0:l0 + LANES]
                vprev = vc_ref[r0 - blk:r0, l0:l0 + LANES]
            k2 = jnp.concatenate([kprev, kc_ref[r0:r0 + blk, l0:l0 + LANES]], axis=0)
            v2 = jnp.concatenate([vprev, vc_ref[r0:r0 + blk, l0:l0 + LANES]], axis=0)
            zq = jnp.zeros_like(qs)
            q_st = jnp.concatenate([jnp.where(head0, qs, zq), jnp.where(head0, zq, qs)], axis=0)
            s = _dot_nt(q_st, k2)
            ps, ms = [], []
            for hh in range(2):
                s_prev = s[hh * blk:(hh + 1) * blk, :blk]
                s_cur = s[hh * blk:(hh + 1) * blk, blk:]
                if i == 0:
                    s_prev = jnp.where(n > 0, s_prev, -1e30)
                s_far = jnp.sum(s_prev * diag_f, axis=1, keepdims=True)
                s_tile = jnp.where(upper, s_prev, s_cur)
                m = jnp.maximum(jnp.max(s_tile, axis=1, keepdims=True), s_far)
                p = jnp.exp2(s_tile - m).astype(BF16)
                p_far = jnp.exp2(s_far - m).astype(BF16)
                ps.append(jnp.concatenate([p * upper_b + p_far * diag_b, p * lower_b], axis=1))
                ms.append(m)
            o2 = _dot(jnp.concatenate(ps, axis=0),
                      jnp.concatenate([v2, ones_v], axis=1))
            o = jnp.where(head0, o2[:blk, :LANES], o2[blk:, :LANES])
            o_ref[r0:r0 + blk, l0:l0 + LANES] = o.astype(BF16)
            for hh in range(2):
                mine = lane == 2 * hp + hh
                m_tile = jnp.where(mine, ms[hh], m_tile)
                l_tile = jnp.where(mine, o2[hh * blk:(hh + 1) * blk, LANES:], l_tile)
        m_ref[r0:r0 + blk, :] = m_tile
        l_ref[r0:r0 + blk, :] = l_tile


def _attn_call(q, k, v, tq=1024):
    b, d, sub, _ = q.shape
    cur = lambda bb, r, n: (bb, r, n, 0)
    prev = lambda bb, r, n: (bb, r, jnp.maximum(n * (tq // ATTN_BLOCK) - 1, 0), 0)
    stat = jax.ShapeDtypeStruct((b, d, sub, LANES), F32)
    return pl.pallas_call(
        functools.partial(_attn_kernel, tq=tq),
        grid=(b, d, sub // tq),
        in_specs=[pl.BlockSpec((None, None, tq, D_MODEL), cur),
                  pl.BlockSpec((None, None, tq, D_MODEL), cur),
                  pl.BlockSpec((None, None, ATTN_BLOCK, D_MODEL), prev),
                  pl.BlockSpec((None, None, tq, D_MODEL), cur),
                  pl.BlockSpec((None, None, ATTN_BLOCK, D_MODEL), prev)],
        out_specs=[pl.BlockSpec((None, None, tq, D_MODEL), cur),
                   pl.BlockSpec((None, None, tq, LANES), cur),
                   pl.BlockSpec((None, None, tq, LANES), cur)],
        out_shape=[jax.ShapeDtypeStruct((b, d, sub, D_MODEL), BF16), stat, stat],
        compiler_params=pltpu.CompilerParams(
            dimension_semantics=("arbitrary", "arbitrary", "arbitrary")),
        name=f"dilated_attn_d{d}",
    )(q, k, k, v, v)


def _rwkv_kernel(at_ref, vb_ref, lt_ref, rt_ref, gate_ref, bonus_ref, pc_ref,
                 lnw_ref, lnb_ref, ones_ref, o_ref, h_ref):
    c_len = CHUNK

    @pl.when(pl.program_id(1) == 0)
    def _():
        h_ref[...] = jnp.zeros_like(h_ref)

    row = lax.broadcasted_iota(jnp.int32, (c_len, 1), 0)
    lane = lax.broadcasted_iota(jnp.int32, (1, LANES), 1)
    head0 = lane < HEAD_DIM

    col = lane % HEAD_DIM
    strict = row > col
    incl = row >= col
    eye = (row == col).astype(F32)
    n_rows = at_ref.shape[0]
    n_chunks = at_ref.shape[1] // c_len
    chains = [(ci, bb, p) for ci in range(n_chunks) for bb in range(n_rows)
              for p in range(N_PAIRS)]
    cut = lambda ref: [ref[bb, ci * c_len:(ci + 1) * c_len, p * LANES:(p + 1) * LANES]
                       for ci, bb, p in chains]
    at, rt, vb = (cut(ref) for ref in (at_ref, rt_ref, vb_ref))
    pcs = [pc_ref[bb, ci, :, p * LANES:(p + 1) * LANES] for ci, bb, p in chains]
    lts = [lt_ref[bb, ci, p] for ci, bb, p in chains]
    top = lambda ts: [t[:c_len] for t in ts]
    bot = lambda ts: [t[c_len:] for t in ts]
    stack = lambda xs, ys: [jnp.concatenate([x, y], axis=0) for x, y in zip(xs, ys)]
    to_bf16 = lambda ts: [t.astype(BF16) for t in ts]

    def bd(y):
        zz = jnp.zeros_like(y)
        return jnp.concatenate([jnp.where(head0, y, zz), jnp.where(head0, zz, y)], axis=0)

    def pmm(xs, ys):
        return [_dot(x, bd(y)) for x, y in zip(xs, ys)]

    def spread(lt):
        sw = pltpu.roll(lt, HEAD_DIM, 1)
        zz = jnp.zeros_like(lt[:c_len])
        w_b = jnp.concatenate([jnp.where(head0, lt[:c_len], zz),
                               jnp.where(head0, zz, sw[c_len:])], axis=0)
        w_k = jnp.concatenate([jnp.where(head0, sw[:c_len], zz),
                               jnp.where(head0, zz, lt[c_len:])], axis=0)
        return jnp.concatenate([w_b, w_k], axis=1)

    a_all = [_dot(lhs, spread(lt))
             for lhs, lt in zip(stack(at, to_bf16(rt)), lts)]
    l_ab = [jnp.where(strict, a[:c_len, :LANES], 0.0) for a in a_all]
    a_ak = [jnp.where(strict, a[:c_len, LANES:], 0.0).astype(BF16) for a in a_all]
    a_rb = [jnp.where(incl, a[c_len:, :LANES], 0.0).astype(BF16) for a in a_all]
    a_rk = [jnp.where(incl, a[c_len:, LANES:], 0.0).astype(BF16) for a in a_all]

    l_b = to_bf16(l_ab)
    s_acc = [eye + l for l in l_ab]
    q_b = to_bf16(pmm(l_b, l_b))
    for _i in range(4):
        res = pmm(stack(to_bf16(s_acc), q_b), q_b)
        s_acc = [s + r_ for s, r_ in zip(s_acc, top(res))]
        q_b = to_bf16(bot(res))
    s_acc = [s + r_ for s, r_ in zip(s_acc, pmm(to_bf16(s_acc), q_b))]
    t_b = to_bf16(s_acc)

    res = pmm(stack(a_ak, a_rk), vb)
    x2_b, yv2 = to_bf16(top(res)), bot(res)

    def pmm2(xs, ys, zs):
        return [_dot(x, jnp.concatenate([bd(y), bd(z)], axis=1)) for x, y, z in zip(xs, ys, zs)]

    res = pmm2(t_b, at, x2_b)
    a_hat_b = to_bf16([r_[:, :LANES] for r_ in res])
    u_v_b = to_bf16([r_[:, LANES:] for r_ in res])
    res = pmm2(a_rb, a_hat_b, u_v_b)
    r_hat = [r_ + d[:, :LANES] for r_, d in zip(rt, res)]
    y_v = [d[:, LANES:] + y2 for d, y2 in zip(res, yv2)]

    mns = [_dot(lt, jnp.concatenate(
        [jnp.concatenate([ah, jnp.zeros_like(ah)], axis=0),
         jnp.concatenate([uv, v_], axis=0)], axis=1))
        for lt, ah, uv, v_ in zip(lts, a_hat_b, u_v_b, vb)]
    m_p = [jnp.where(head0, mn[:c_len, :LANES], mn[c_len:, :LANES]) + eye for mn in mns]
    n_p = [jnp.where(head0, mn[:c_len, LANES:], mn[c_len:, LANES:]) for mn in mns]

    def decay_tile(pc_row):
        per_row = jnp.broadcast_to(pc_row, (LANES, LANES)).T
        return jnp.where(head0, per_row[:c_len], per_row[c_len:])

    decay = [decay_tile(pc_) for pc_ in pcs]

    lhs = stack(to_bf16(m_p), to_bf16(r_hat))
    states = [h_ref[bb, p] for bb in range(n_rows) for p in range(N_PAIRS)]
    per_chunk = n_rows * N_PAIRS
    ones_bd = ones_ref[...]
    for ci in range(n_chunks):
        sel = slice(ci * per_chunk, (ci + 1) * per_chunk)
        res = pmm(lhs[sel], to_bf16(states))
        states = [(r_[:c_len] + n_) * d_ for r_, n_, d_ in zip(res, n_p[sel], decay[sel])]
        ys = [r_[c_len:] + yv for r_, yv in zip(res, y_v[sel])]
        rows = slice(ci * c_len, (ci + 1) * c_len)
        for bb in range(n_rows):
            y = jnp.concatenate(ys[bb * N_PAIRS:(bb + 1) * N_PAIRS], axis=1)
            mean = _headsum(y, ones_bd) * (1.0 / HEAD_DIM)
            yc = y - mean
            var = _headsum(yc * yc, ones_bd) * (1.0 / HEAD_DIM)
            yn = yc * lax.rsqrt(var + GN_EPS) * lnw_ref[...] + lnb_ref[...]
            o_ref[bb, rows, :] = ((yn + bonus_ref[bb, rows, :]) * gate_ref[bb, rows, :]).astype(BF16)
    for j, h_new in enumerate(states):
        h_ref[j // N_PAIRS, j % N_PAIRS] = h_new


def _rwkv_call(feats, ln_w, ln_b, n_chunks=4):
    at, vb, lt, rt, gate, bonus, pc = feats
    b, s, _ = at.shape
    nb = 2 if b % 2 == 0 else 1
    span = n_chunks * CHUNK
    const = lambda bb, t: (0, 0)
    vec = pl.BlockSpec((1, D_MODEL), const)
    tok = pl.BlockSpec((nb, span, D_MODEL), lambda bb, t: (bb, t, 0))
    return pl.pallas_call(
        _rwkv_kernel,
        grid=(b // nb, s // span),
        in_specs=[tok, tok,
                  pl.BlockSpec((nb, n_chunks, N_PAIRS, LANES, LANES),
                               lambda bb, t: (bb, t, 0, 0, 0)),
                  tok, tok, tok,
                  pl.BlockSpec((nb, n_chunks, 1, D_MODEL), lambda bb, t: (bb, t, 0, 0)),
                  vec, vec, pl.BlockSpec((GROUP, GROUP), const)],
        out_specs=tok,
        out_shape=jax.ShapeDtypeStruct((b, s, D_MODEL), BF16),
        scratch_shapes=[pltpu.VMEM((nb, N_PAIRS, CHUNK, LANES), F32)],
        compiler_params=pltpu.CompilerParams(dimension_semantics=("arbitrary", "arbitrary")),
        name="rwkv7_mixer",
    )(at, vb, lt, rt, gate, bonus, pc, ln_w, ln_b, _head_ones())


def _merge_kernel(x_ref, g_ref, wg_ref, o1_ref, o4_ref, o16_ref, m1_ref, m4_ref, m16_ref,
                  l1_ref, l4_ref, l16_ref, ob_ref, pa_ref, pb_ref, wo_ref, e_ref, out_ref,
                  o_scr, stat_scr):
    x = x_ref[0]
    h = _rmsnorm(x, g_ref[...]).astype(BF16)
    gl = _dot(h, wg_ref[...])
    g_a = jax.nn.sigmoid(gl[:, :D_MODEL])
    g_b = jax.nn.sigmoid(gl[:, D_MODEL:])

    def natural_order(src_ref, dst_ref):
        _, d, rows, width = src_ref.shape
        for r in range(d):
            for cblk in range(width // LANES):
                val = src_ref[0, r, :, cblk * LANES:(cblk + 1) * LANES].astype(F32)
                if d == 1:
                    dst_ref[cblk] = val
                else:
                    dst_ref[cblk, pl.ds(r, rows, stride=d), :] = val

    ms, ls = [], []
    for gi, (m_ref, l_ref) in enumerate(((m1_ref, l1_ref), (m4_ref, l4_ref), (m16_ref, l16_ref))):
        natural_order(m_ref, stat_scr.at[2 * gi])
        natural_order(l_ref, stat_scr.at[2 * gi + 1])
        ms.append(stat_scr[2 * gi, 0])
        ls.append(stat_scr[2 * gi + 1, 0])
    mx = jnp.maximum(jnp.maximum(ms[0], ms[1]), ms[2])
    es = [jnp.exp2(m - mx) for m in ms]
    den = es[0] * ls[0] + es[1] * ls[1] + es[2] * ls[2]
    expand = e_ref[...]
    o_a = jnp.zeros(x.shape, F32)
    for e, o_ref in zip(es, (o1_ref, o4_ref, o16_ref)):
        w = e / den
        hi = w.astype(BF16)
        lo = (w - hi.astype(F32)).astype(BF16)
        natural_order(o_ref, o_scr)
        o_g = jnp.concatenate([o_scr[p] for p in range(N_PAIRS)], axis=1)
        o_a = o_a + _dot(jnp.concatenate([hi, lo], axis=1), expand) * o_g

    merged = (g_a * _dot(o_a.astype(BF16), pa_ref[...])
              + g_b * _dot(ob_ref[0], pb_ref[...]))
    out_ref[0] = x + _dot(merged.astype(BF16), wo_ref[...])


def _merge_call(x, g, wg, os_, ms, ls, ob, pa, pb, wo, tm=256):
    b, s, _ = x.shape
    expand = jnp.kron(jnp.eye(N_HEADS, dtype=F32), jnp.ones((1, HEAD_DIM), F32))
    expand = jnp.concatenate([expand, jnp.zeros((LANES - N_HEADS, D_MODEL), F32)], 0).astype(BF16)
    expand = jnp.concatenate([expand, expand], axis=0)
    const = lambda bb, i: (0, 0)
    wide = pl.BlockSpec((1, tm, D_MODEL), lambda bb, i: (bb, i, 0))
    sq = pl.BlockSpec((D_MODEL, D_MODEL), const)
    res = lambda a: pl.BlockSpec((1, a.shape[1], tm // a.shape[1], a.shape[3]),
                                 lambda bb, i: (bb, 0, i, 0))
    return pl.pallas_call(
        _merge_kernel,
        grid=(b, s // tm),
        in_specs=[wide, pl.BlockSpec((1, D_MODEL), const),
                  pl.BlockSpec((D_MODEL, 2 * D_MODEL), const),
                  *[res(a) for a in (*os_, *ms, *ls)], wide, sq, sq, sq,
                  pl.BlockSpec((2 * LANES, D_MODEL), const)],
        out_specs=wide,
        out_shape=jax.ShapeDtypeStruct((b, s, D_MODEL), F32),
        scratch_shapes=[pltpu.VMEM((N_PAIRS, tm, LANES), F32),
                        pltpu.VMEM((len(ms) + len(ls), 1, tm, LANES), F32)],
        compiler_params=pltpu.CompilerParams(dimension_semantics=("arbitrary", "arbitrary")),
        name="merge_proj",
    )(x, g, wg, *os_, *ms, *ls, ob, pa, pb, wo, expand)


def _ffn_kernel(x_ref, g_ref, wg_ref, wu_ref, wd_ref, gf_ref, out_ref, *, final_norm):
    x = x_ref[...]
    h = _rmsnorm(x, g_ref[...]).astype(BF16)
    act = (jax.nn.silu(_dot(h, wg_ref[...])) * _dot(h, wu_ref[...])).astype(BF16)
    x2 = x + _dot(act, wd_ref[...])
    out_ref[...] = _rmsnorm(x2, gf_ref[...]) if final_norm else x2


def _ffn_call(x2, g, wg, wu, wd, gf, final_norm, tm=512):
    t = x2.shape[0]
    row = lambda i: (i, 0)
    const = lambda i: (0, 0)
    vec = pl.BlockSpec((1, D_MODEL), const)
    resident = lambda shape: pl.BlockSpec(shape, const, pipeline_mode=pl.Buffered(1))
    return pl.pallas_call(
        functools.partial(_ffn_kernel, final_norm=final_norm),
        grid=(t // tm,),
        in_specs=[pl.BlockSpec((tm, D_MODEL), row), vec,
                  resident((D_MODEL, D_FF)), resident((D_MODEL, D_FF)),
                  resident((D_FF, D_MODEL)), vec],
        out_specs=pl.BlockSpec((tm, D_MODEL), row),
        out_shape=jax.ShapeDtypeStruct((t, D_MODEL), F32),
        compiler_params=pltpu.CompilerParams(dimension_semantics=("arbitrary",)),
        name="ffn_final",
    )(x2, g, wg, wu, wd, gf)


def _rotary_tables(seq):
    half = ROPE_DIM // 2
    in_head = np.arange(LANES) % HEAD_DIM
    inv_freq = ROPE_THETA ** (-(in_head % half).astype(np.float64) * (2.0 / ROPE_DIM))
    ang = np.arange(seq, dtype=np.float64)[:, None] * inv_freq[None, :]
    cos, sin = np.cos(ang), np.sin(ang)
    first = (in_head < half)[None, :]
    second = ((in_head >= half) & (in_head < ROPE_DIM))[None, :]
    c = np.where(first | second, cos, 1.0)
    s1 = np.where(second, sin, 0.0)
    s2 = np.where(first, -sin, 0.0)
    return tuple(jnp.asarray(t.astype(np.float32)) for t in (c, s1, s2))


def kernel(x, norm_mix_g, w_in, shift_mu, decay_w0, decay_w2, iclr_a0, iclr_a2, gate_g2, k_k, k_a, r_k, ln_x_w, ln_x_b, proj_attn, proj_rwkv, w_out, norm_ffn_g, ffn_w_gate, ffn_w_up, ffn_w_down, norm_final_g):
    b, s, d = x.shape
    assert d == D_MODEL and s % (max(dil for _, dil in DILATED_GROUPS) * 1024) == 0
    depth = w_in.shape[0]
    c, s1, s2 = _rotary_tables(s)
    vec = lambda a: a.reshape(1, -1).astype(F32)
    n_attn = 3 * D_MODEL
    for l in range(depth):
        g_mix = vec(norm_mix_g[l])
        w = w_in[l]
        qkv = _qkv_call(x, g_mix, w[:, :n_attn].astype(BF16), c, s1, s2)
        zero = jnp.zeros((DECAY_LORA, D_MODEL), F32)
        w2a2 = jnp.concatenate(
            [jnp.concatenate([decay_w2[l], zero], axis=1),
             jnp.concatenate([zero, iclr_a2[l]], axis=1)], axis=0).astype(BF16)
        feats = _rwkv_feat_call(x, g_mix, w[:, n_attn:n_attn + D_SHIFTED].astype(BF16),
                                vec(shift_mu[l]), vec(decay_w0[l]), w2a2, vec(iclr_a0[l]),
                                gate_g2[l].astype(BF16), vec(k_k[l]), vec(k_a[l]), vec(r_k[l]))
        os_, ms, ls = [], [], []
        for gi in range(len(DILATED_GROUPS)):
            o_g, m_g, l_g = _attn_call(*qkv[3 * gi:3 * gi + 3])
            os_.append(o_g)
            ms.append(m_g)
            ls.append(l_g)
        o_b = _rwkv_call(feats, vec(ln_x_w[l]), vec(ln_x_b[l]))
        x = _merge_call(x, g_mix, w[:, n_attn + D_SHIFTED:].astype(BF16), os_, ms, ls, o_b,
                        proj_attn[l].astype(BF16), proj_rwkv[l].astype(BF16),
                        w_out[l].astype(BF16))
        x = _ffn_call(x.reshape(b * s, d), vec(norm_ffn_g[l]), ffn_w_gate[l].astype(BF16),
                      ffn_w_up[l].astype(BF16), ffn_w_down[l].astype(BF16),
                      vec(norm_final_g), final_norm=(l == depth - 1)).reshape(b, s, d)
    return x
```

```python
import functools

import jax
import jax.numpy as jnp
import numpy as np
from jax import lax
from jax.experimental import pallas as pl
from jax.experimental.pallas import tpu as pltpu

F32 = jnp.float32
BF16 = jnp.bfloat16

D_MODEL = 1024
HEAD_DIM = 64
N_HEADS = 16
ROPE_DIM = 16
ROPE_THETA = 500000.0
DILATED_GROUPS = ((128, 1), (512, 4), (2048, 16))
ATTN_BLOCK = 128
DECAY_LORA = 64
ICLR_LORA = 64
GATE_LORA = 128
D_FF = 2816
RMS_EPS = 1e-6
GN_EPS = 64e-5
D_SHIFTED = 3 * D_MODEL + DECAY_LORA + ICLR_LORA + GATE_LORA

LANES = 128
SUBLANES = 8
N_PAIRS = D_MODEL // LANES
CHUNK = 64
GROUP = 256

LOG2_E = 1.4426950408889634
NT_DIMS = (((1,), (1,)), ((), ()))


def _dot(a, b):
    return jnp.dot(a, b, preferred_element_type=F32)


def _dot_nt(a, b):
    return lax.dot_general(a, b, NT_DIMS, preferred_element_type=F32)


def _rmsnorm(x, g):
    return x * lax.rsqrt(jnp.mean(x * x, axis=-1, keepdims=True) + RMS_EPS) * g


def _qkv_kernel(x_ref, g_ref, w_ref, c_ref, s1_ref, s2_ref, *refs, tm):
    outs, stages = refs[:-2], refs[-2:]
    dilations = [d for _, d in DILATED_GROUPS]
    assert dilations[0] == 1
    h = _rmsnorm(x_ref[0], g_ref[...]).astype(BF16)
    p = _dot(h, w_ref[...])
    c, s1, s2 = c_ref[...], s1_ref[...], s2_ref[...]

    def rot(t):
        return t * c + pltpu.roll(t, 8, 1) * s1 + pltpu.roll(t, LANES - 8, 1) * s2

    for blk in range(N_PAIRS):
        lanes = slice(blk * LANES, (blk + 1) * LANES)
        vals = (rot(p[:, lanes]) * (HEAD_DIM ** -0.5 * LOG2_E),
                rot(p[:, D_MODEL + blk * LANES:D_MODEL + (blk + 1) * LANES]),
                p[:, 2 * D_MODEL + blk * LANES:2 * D_MODEL + (blk + 1) * LANES])
        for j, val in enumerate(vals):
            stages[0][j * N_PAIRS + blk] = val
            outs[j][0, 0, :, lanes] = val.astype(BF16)

    d_prev = 1
    for gi in range(1, len(dilations)):
        d = dilations[gi]
        step = d // d_prev
        assert step * d_prev == d
        rows_prev, rows = tm // d_prev, tm // d
        src_ref, dst_ref = stages[(gi - 1) % 2], stages[gi % 2]
        for j in range(3):
            for blk in range(N_PAIRS):
                slab = j * N_PAIRS + blk
                lanes = slice(blk * LANES, (blk + 1) * LANES)
                for r_prev in range(d_prev):
                    for off in range(step):
                        r = d_prev * off + r_prev
                        val = src_ref[slab, pl.ds(r_prev * rows_prev + off, rows, stride=step), :]
                        outs[3 * gi + j][0, r, :, lanes] = val.astype(BF16)
                        if gi + 1 < len(dilations):
                            dst_ref[slab, r * rows:(r + 1) * rows, :] = val
        d_prev = d


def _qkv_call(x, g, w, c, s1, s2, tm=512):
    b, s, _ = x.shape
    const = lambda bb, i: (0, 0)
    pos = lambda bb, i: (i, 0)
    out_specs, out_shape = [], []
    for _, d in DILATED_GROUPS:
        for _j in range(3):
            out_specs.append(pl.BlockSpec((1, d, tm // d, D_MODEL), lambda bb, i: (bb, 0, i, 0)))
            out_shape.append(jax.ShapeDtypeStruct((b, d, s // d, D_MODEL), BF16))
    return pl.pallas_call(
        functools.partial(_qkv_kernel, tm=tm),
        grid=(b, s // tm),
        in_specs=[pl.BlockSpec((1, tm, D_MODEL), lambda bb, i: (bb, i, 0)),
                  pl.BlockSpec((1, D_MODEL), const),
                  pl.BlockSpec((D_MODEL, 3 * D_MODEL), const),
                  pl.BlockSpec((tm, LANES), pos), pl.BlockSpec((tm, LANES), pos),
                  pl.BlockSpec((tm, LANES), pos)],
        out_specs=out_specs,
        out_shape=out_shape,
        scratch_shapes=[pltpu.VMEM((3 * N_PAIRS, tm, LANES), F32),
                        pltpu.VMEM((3 * N_PAIRS, tm, LANES), F32)],
        compiler_params=pltpu.CompilerParams(dimension_semantics=("arbitrary", "arbitrary")),
        name="qkv_proj",
    )(x, g, w, c, s1, s2)


def _rwkv_feat_kernel(x_ref, g_ref, w_ref, mu_ref, w0_ref, w2a2_ref, a0_ref, g2_ref, kk_ref,
                      ka_ref, rk_ref, ones_ref,
                      at_ref, vb_ref, lt_ref, rt_ref, gate_ref, bonus_ref, pc_ref,
                      carry_ref, *, tm):
    @pl.when(pl.program_id(1) == 0)
    def _():
        carry_ref[...] = jnp.zeros_like(carry_ref)

    lane = lax.broadcasted_iota(jnp.int32, (1, LANES), 1)
    head0 = lane < HEAD_DIM
    ones_bd = ones_ref[...]

    h = _rmsnorm(x_ref[0], g_ref[...]).astype(BF16)
    cols = _dot(h, w_ref[...])
    row8 = lax.broadcasted_iota(jnp.int32, (SUBLANES, 1), 0)
    carry = carry_ref[...]
    carry_ref[...] = cols[tm - 1:tm, :]
    mu = mu_ref[...]
    one_minus_mu = 1.0 - mu

    def shifted(ci, lanes):
        cur = cols[ci * CHUNK:(ci + 1) * CHUNK, lanes]
        last = carry[:, lanes] if ci == 0 else cols[ci * CHUNK - 1:ci * CHUNK, lanes]
        prev = pltpu.roll(cur, 1, 0)
        prev = jnp.concatenate([jnp.where(row8 == 0, last, prev[:SUBLANES]),
                                prev[SUBLANES:]], axis=0)
        return cur * one_minus_mu[:, lanes] + prev * mu[:, lanes]

    def prefix_sum(x):
        outs, total = [], None
        for gi in range(x.shape[0] // SUBLANES):
            xg = x[gi * SUBLANES:(gi + 1) * SUBLANES]
            for sh in (1, 2, 4):
                xg = xg + jnp.where(row8 >= sh, pltpu.roll(xg, sh, 0), 0.0)
            if total is not None:
                xg = xg + total
            total = xg[SUBLANES - 1:SUBLANES]
            outs.append(xg)
        return jnp.concatenate(outs, axis=0)

    w0, a0, k_k, k_a, r_k = (ref[...] for ref in (w0_ref, a0_ref, kk_ref, ka_ref, rk_ref))
    pairs_per_group = GROUP // LANES
    for ci in range(tm // CHUNK):
        rows = slice(ci * CHUNK, (ci + 1) * CHUNK)
        tail = shifted(ci, slice(3 * D_MODEL, D_SHIFTED))
        slab, g_lo = tail[:, :LANES], tail[:, LANES:]
        z = jnp.where(head0, jnp.tanh(slab), slab).astype(BF16)
        lora = _dot(z, w2a2_ref[...])
        gate_ref[0, rows, :] = _dot(jax.nn.sigmoid(g_lo).astype(BF16), g2_ref[...])
        for gi in range(D_MODEL // GROUP):
            lanes = slice(gi * GROUP, (gi + 1) * GROUP)
            at = lambda base: slice(base + gi * GROUP, base + (gi + 1) * GROUP)
            r, k, v = shifted(ci, at(0)), shifted(ci, at(D_MODEL)), shifted(ci, at(2 * D_MODEL))
            lw = jax.nn.sigmoid(w0[:, lanes] + lora[:, lanes]) * (-(jnp.e ** -0.5))
            eta = jax.nn.sigmoid(a0[:, lanes] + lora[:, at(D_MODEL)])
            kk = k * k_k[:, lanes]
            kk = kk * jnp.minimum(lax.rsqrt(_headsum(kk * kk, ones_bd)), 1e12)
            k_mod = k * (1.0 + (eta - 1.0) * k_a[:, lanes])
            bonus_ref[0, rows, lanes] = _headsum(r * k_mod * r_k[:, lanes], ones_bd) * v
            b_s = kk * eta

            cl = prefix_sum(lw)
            e_ncl = jnp.exp(-cl)
            at_ref[0, rows, lanes] = (-kk * jnp.exp(cl - lw)).astype(BF16)
            rt_ref[0, rows, lanes] = r * jnp.exp(cl)
            vb_ref[0, rows, lanes] = v.astype(BF16)
            pc_ref[0, ci, :, lanes] = jnp.exp(cl[CHUNK - 1:CHUNK, :])
            b_t = b_s * e_ncl
            k_t = k_mod * e_ncl
            for pp in range(pairs_per_group):
                sl = slice(pp * LANES, (pp + 1) * LANES)
                lt_ref[0, ci, gi * pairs_per_group + pp] = jnp.concatenate(
                    [b_t[:, sl], k_t[:, sl]], axis=0).T.astype(BF16)


def _headsum(x, ones_bd):
    parts = []
    for gi in range(x.shape[1] // GROUP):
        xg = x[:, gi * GROUP:(gi + 1) * GROUP]
        hi = xg.astype(BF16)
        lo = (xg - hi.astype(F32)).astype(BF16)
        parts.append(_dot(hi, ones_bd) + _dot(lo, ones_bd))
    return jnp.concatenate(parts, axis=1)


def _head_ones():
    return jnp.kron(jnp.eye(GROUP // HEAD_DIM, dtype=F32),
                    jnp.ones((HEAD_DIM, HEAD_DIM), F32)).astype(BF16)


def _rwkv_feat_call(x, g, w, mu, w0, w2a2, a0, g2, k_k, k_a, r_k, tm=512):
    b, s, _ = x.shape
    n_chunks = tm // CHUNK
    const = lambda bb, i: (0, 0)
    vec = pl.BlockSpec((1, D_MODEL), const)
    tok = lambda bb, i: (bb, i, 0)
    tok_spec = pl.BlockSpec((1, tm, D_MODEL), tok)
    tok_shape = lambda dt: jax.ShapeDtypeStruct((b, s, D_MODEL), dt)
    return pl.pallas_call(
        functools.partial(_rwkv_feat_kernel, tm=tm),
        grid=(b, s // tm),
        in_specs=[tok_spec, vec,
                  pl.BlockSpec((D_MODEL, D_SHIFTED), const),
                  pl.BlockSpec((1, D_SHIFTED), const), vec,
                  pl.BlockSpec((LANES, 2 * D_MODEL), const), vec,
                  pl.BlockSpec((GATE_LORA, D_MODEL), const), vec, vec, vec,
                  pl.BlockSpec((GROUP, GROUP), const)],
        out_specs=[tok_spec, tok_spec,
                   pl.BlockSpec((1, n_chunks, N_PAIRS, LANES, LANES),
                                lambda bb, i: (bb, i, 0, 0, 0)),
                   tok_spec, tok_spec, tok_spec,
                   pl.BlockSpec((1, n_chunks, 1, D_MODEL), lambda bb, i: (bb, i, 0, 0))],
        out_shape=[tok_shape(BF16), tok_shape(BF16),
                   jax.ShapeDtypeStruct((b, s // CHUNK, N_PAIRS, LANES, LANES), BF16),
                   tok_shape(F32), tok_shape(F32), tok_shape(F32),
                   jax.ShapeDtypeStruct((b, s // CHUNK, 1, D_MODEL), F32)],
        scratch_shapes=[pltpu.VMEM((1, D_SHIFTED), F32)],
        compiler_params=pltpu.CompilerParams(dimension_semantics=("arbitrary", "arbitrary")),
        name="rwkv_feat",
    )(x, g, w, mu, w0, w2a2, a0, g2, k_k, k_a, r_k, _head_ones())


def _attn_kernel(q_ref, kc_ref, kp_ref, vc_ref, vp_ref, o_ref, m_ref, l_ref, *, tq):
    n = pl.program_id(2)
    blk = ATTN_BLOCK
    qi = lax.broadcasted_iota(jnp.int32, (blk, 1), 0)
    kc = lax.broadcasted_iota(jnp.int32, (1, blk), 1)
    upper = kc > qi
    diag_f = (kc == qi).astype(F32)
    upper_b = upper.astype(BF16)
    lower_b = (kc <= qi).astype(BF16)
    diag_b = diag_f.astype(BF16)
    lane = lax.broadcasted_iota(jnp.int32, (1, LANES), 1)
    head0 = lane < HEAD_DIM
    ones_v = jnp.ones((2 * blk, LANES), BF16)

    for i in range(tq // blk):
        r0 = i * blk
        m_tile = jnp.zeros((blk, LANES), F32)
        l_tile = jnp.ones((blk, LANES), F32)
        for hp in range(N_PAIRS):
            l0 = hp * LANES
            qs = q_ref[r0:r0 + blk, l0:l0 + LANES]
            if i == 0:
                kprev = kp_ref[:, l0:l0 + LANES]
                vprev = vp_ref[:, l0:l0 + LANES]
            else:
                kprev = kc_ref[r0 - blk:r0, l0:l0 + LANES]
                vprev = vc_ref[r0 - blk:r0, l0:l0 + LANES]
            k2 = jnp.concatenate([kprev, kc_ref[r0:r0 + blk, l0:l0 + LANES]], axis=0)
            v2 = jnp.concatenate([vprev, vc_ref[r0:r0 + blk, l0:l0 + LANES]], axis=0)
            zq = jnp.zeros_like(qs)
            q_st = jnp.concatenate([jnp.where(head0, qs, zq), jnp.where(head0, zq, qs)], axis=0)
            s = _dot_nt(q_st, k2)
            ps, ms = [], []
            for hh in range(2):
                s_prev = s[hh * blk:(hh + 1) * blk, :blk]
                s_cur = s[hh * blk:(hh + 1) * blk, blk:]
                if i == 0:
                    s_prev = jnp.where(n > 0, s_prev, -1e30)
                s_far = jnp.sum(s_prev * diag_f, axis=1, keepdims=True)
                s_tile = jnp.where(upper, s_prev, s_cur)
                m = jnp.maximum(jnp.max(s_tile, axis=1, keepdims=True), s_far)
                p = jnp.exp2(s_tile - m).astype(BF16)
                p_far = jnp.exp2(s_far - m).astype(BF16)
                ps.append(jnp.concatenate([p * upper_b + p_far * diag_b, p * lower_b], axis=1))
                ms.append(m)
            o2 = _dot(jnp.concatenate(ps, axis=0),
                      jnp.concatenate([v2, ones_v], axis=1))
            o = jnp.where(head0, o2[:blk, :LANES], o2[blk:, :LANES])
            o_ref[r0:r0 + blk, l0:l0 + LANES] = o.astype(BF16)
            for hh in range(2):
                mine = lane == 2 * hp + hh
                m_tile = jnp.where(mine, ms[hh], m_tile)
                l_tile = jnp.where(mine, o2[hh * blk:(hh + 1) * blk, LANES:], l_tile)
        m_ref[r0:r0 + blk, :] = m_tile
        l_ref[r0:r0 + blk, :] = l_tile


def _attn_call(q, k, v, tq=1024):
    b, d, sub, _ = q.shape
    cur = lambda bb, r, n: (bb, r, n, 0)
    prev = lambda bb, r, n: (bb, r, jnp.maximum(n * (tq // ATTN_BLOCK) - 1, 0), 0)
    stat = jax.ShapeDtypeStruct((b, d, sub, LANES), F32)
    return pl.pallas_call(
        functools.partial(_attn_kernel, tq=tq),
        grid=(b, d, sub // tq),
        in_specs=[pl.BlockSpec((None, None, tq, D_MODEL), cur),
                  pl.BlockSpec((None, None, tq, D_MODEL), cur),
                  pl.BlockSpec((None, None, ATTN_BLOCK, D_MODEL), prev),
                  pl.BlockSpec((None, None, tq, D_MODEL), cur),
                  pl.BlockSpec((None, None, ATTN_BLOCK, D_MODEL), prev)],
        out_specs=[pl.BlockSpec((None, None, tq, D_MODEL), cur),
                   pl.BlockSpec((None, None, tq, LANES), cur),
                   pl.BlockSpec((None, None, tq, LANES), cur)],
        out_shape=[jax.ShapeDtypeStruct((b, d, sub, D_MODEL), BF16), stat, stat],
        compiler_params=pltpu.CompilerParams(
            dimension_semantics=("arbitrary", "arbitrary", "arbitrary")),
        name=f"dilated_attn_d{d}",
    )(q, k, k, v, v)


def _rwkv_kernel(at_ref, vb_ref, lt_ref, rt_ref, gate_ref, bonus_ref, pc_ref,
                 lnw_ref, lnb_ref, ones_ref, o_ref, h_ref):
    c_len = CHUNK

    @pl.when(pl.program_id(1) == 0)
    def _():
        h_ref[...] = jnp.zeros_like(h_ref)

    row = lax.broadcasted_iota(jnp.int32, (c_len, 1), 0)
    lane = lax.broadcasted_iota(jnp.int32, (1, LANES), 1)
    head0 = lane < HEAD_DIM

    col = lane % HEAD_DIM
    strict = row > col
    incl = row >= col
    eye = (row == col).astype(F32)
    n_rows = at_ref.shape[0]
    n_chunks = at_ref.shape[1] // c_len
    chains = [(ci, bb, p) for ci in range(n_chunks) for bb in range(n_rows)
              for p in range(N_PAIRS)]
    cut = lambda ref: [ref[bb, ci * c_len:(ci + 1) * c_len, p * LANES:(p + 1) * LANES]
                       for ci, bb, p in chains]
    at, rt, vb = (cut(ref) for ref in (at_ref, rt_ref, vb_ref))
    pcs = [pc_ref[bb, ci, :, p * LANES:(p + 1) * LANES] for ci, bb, p in chains]
    lts = [lt_ref[bb, ci, p] for ci, bb, p in chains]
    top = lambda ts: [t[:c_len] for t in ts]
    bot = lambda ts: [t[c_len:] for t in ts]
    stack = lambda xs, ys: [jnp.concatenate([x, y], axis=0) for x, y in zip(xs, ys)]
    to_bf16 = lambda ts: [t.astype(BF16) for t in ts]

    def bd(y):
        zz = jnp.zeros_like(y)
        return jnp.concatenate([jnp.where(head0, y, zz), jnp.where(head0, zz, y)], axis=0)

    def pmm(xs, ys):
        return [_dot(x, bd(y)) for x, y in zip(xs, ys)]

    def spread(lt):
        sw = pltpu.roll(lt, HEAD_DIM, 1)
        zz = jnp.zeros_like(lt[:c_len])
        w_b = jnp.concatenate([jnp.where(head0, lt[:c_len], zz),
                               jnp.where(head0, zz, sw[c_len:])], axis=0)
        w_k = jnp.concatenate([jnp.where(head0, sw[:c_len], zz),
                               jnp.where(head0, zz, lt[c_len:])], axis=0)
        return jnp.concatenate([w_b, w_k], axis=1)

    a_all = [_dot(lhs, spread(lt))
             for lhs, lt in zip(stack(at, to_bf16(rt)), lts)]
    l_ab = [jnp.where(strict, a[:c_len, :LANES], 0.0) for a in a_all]
    a_ak = [jnp.where(strict, a[:c_len, LANES:], 0.0).astype(BF16) for a in a_all]
    a_rb = [jnp.where(incl, a[c_len:, :LANES], 0.0).astype(BF16) for a in a_all]
    a_rk = [jnp.where(incl, a[c_len:, LANES:], 0.0).astype(BF16) for a in a_all]

    l_b = to_bf16(l_ab)
    s_acc = [eye + l for l in l_ab]
    q_b = to_bf16(pmm(l_b, l_b))
    for _i in range(4):
        res = pmm(stack(to_bf16(s_acc), q_b), q_b)
        s_acc = [s + r_ for s, r_ in zip(s_acc, top(res))]
        q_b = to_bf16(bot(res))
    s_acc = [s + r_ for s, r_ in zip(s_acc, pmm(to_bf16(s_acc), q_b))]
    t_b = to_bf16(s_acc)

    res = pmm(stack(a_ak, a_rk), vb)
    x2_b, yv2 = to_bf16(top(res)), bot(res)

    def pmm2(xs, ys, zs):
        return [_dot(x, jnp.concatenate([bd(y), bd(z)], axis=1)) for x, y, z in zip(xs, ys, zs)]

    res = pmm2(t_b, at, x2_b)
    a_hat_b = to_bf16([r_[:, :LANES] for r_ in res])
    u_v_b = to_bf16([r_[:, LANES:] for r_ in res])
    res = pmm2(a_rb, a_hat_b, u_v_b)
    r_hat = [r_ + d[:, :LANES] for r_, d in zip(rt, res)]
    y_v = [d[:, LANES:] + y2 for d, y2 in zip(res, yv2)]

    mns = [_dot(lt, jnp.concatenate(
        [jnp.concatenate([ah, jnp.zeros_like(ah)], axis=0),
         jnp.concatenate([uv, v_], axis=0)], axis=1))
        for lt, ah, uv, v_ in zip(lts, a_hat_b, u_v_b, vb)]
    m_p = [jnp.where(head0, mn[:c_len, :LANES], mn[c_len:, :LANES]) + eye for mn in mns]
    n_p = [jnp.where(head0, mn[:c_len, LANES:], mn[c_len:, LANES:]) for mn in mns]

    def decay_tile(pc_row):
        per_row = jnp.broadcast_to(pc_row, (LANES, LANES)).T
        return jnp.where(head0, per_row[:c_len], per_row[c_len:])

    decay = [decay_tile(pc_) for pc_ in pcs]

    lhs = stack(to_bf16(m_p), to_bf16(r_hat))
    states = [h_ref[bb, p] for bb in range(n_rows) for p in range(N_PAIRS)]
    per_chunk = n_rows * N_PAIRS
    ones_bd = ones_ref[...]
    for ci in range(n_chunks):
        sel = slice(ci * per_chunk, (ci + 1) * per_chunk)
        res = pmm(lhs[sel], to_bf16(states))
        states = [(r_[:c_len] + n_) * d_ for r_, n_, d_ in zip(res, n_p[sel], decay[sel])]
        ys = [r_[c_len:] + yv for r_, yv in zip(res, y_v[sel])]
        rows = slice(ci * c_len, (ci + 1) * c_len)
        for bb in range(n_rows):
            y = jnp.concatenate(ys[bb * N_PAIRS:(bb + 1) * N_PAIRS], axis=1)
            mean = _headsum(y, ones_bd) * (1.0 / HEAD_DIM)
            yc = y - mean
            var = _headsum(yc * yc, ones_bd) * (1.0 / HEAD_DIM)
            yn = yc * lax.rsqrt(var + GN_EPS) * lnw_ref[...] + lnb_ref[...]
            o_ref[bb, rows, :] = ((yn + bonus_ref[bb, rows, :]) * gate_ref[bb, rows, :]).astype(BF16)
    for j, h_new in enumerate(states):
        h_ref[j // N_PAIRS, j % N_PAIRS] = h_new


def _rwkv_call(feats, ln_w, ln_b, n_chunks=4):
    at, vb, lt, rt, gate, bonus, pc = feats
    b, s, _ = at.shape
    nb = 2 if b % 2 == 0 else 1
    span = n_chunks * CHUNK
    const = lambda bb, t: (0, 0)
    vec = pl.BlockSpec((1, D_MODEL), const)
    tok = pl.BlockSpec((nb, span, D_MODEL), lambda bb, t: (bb, t, 0))
    return pl.pallas_call(
        _rwkv_kernel,
        grid=(b // nb, s // span),
        in_specs=[tok, tok,
                  pl.BlockSpec((nb, n_chunks, N_PAIRS, LANES, LANES),
                               lambda bb, t: (bb, t, 0, 0, 0)),
                  tok, tok, tok,
                  pl.BlockSpec((nb, n_chunks, 1, D_MODEL), lambda bb, t: (bb, t, 0, 0)),
                  vec, vec, pl.BlockSpec((GROUP, GROUP), const)],
        out_specs=tok,
        out_shape=jax.ShapeDtypeStruct((b, s, D_MODEL), BF16),
        scratch_shapes=[pltpu.VMEM((nb, N_PAIRS, CHUNK, LANES), F32)],
        compiler_params=pltpu.CompilerParams(dimension_semantics=("arbitrary", "arbitrary")),
        name="rwkv7_mixer",
    )(at, vb, lt, rt, gate, bonus, pc, ln_w, ln_b, _head_ones())


def _merge_kernel(x_ref, g_ref, wg_ref, o1_ref, o4_ref, o16_ref, m1_ref, m4_ref, m16_ref,
                  l1_ref, l4_ref, l16_ref, ob_ref, pa_ref, pb_ref, wo_ref, e_ref, out_ref,
                  o_scr, stat_scr):
    x = x_ref[0]
    h = _rmsnorm(x, g_ref[...]).astype(BF16)
    gl = _dot(h, wg_ref[...])
    g_a = jax.nn.sigmoid(gl[:, :D_MODEL])
    g_b = jax.nn.sigmoid(gl[:, D_MODEL:])

    def natural_order(src_ref, dst_ref):
        _, d, rows, width = src_ref.shape
        for r in range(d):
            for cblk in range(width // LANES):
                val = src_ref[0, r, :, cblk * LANES:(cblk + 1) * LANES].astype(F32)
                if d == 1:
                    dst_ref[cblk] = val
                else:
                    dst_ref[cblk, pl.ds(r, rows, stride=d), :] = val

    ms, ls = [], []
    for gi, (m_ref, l_ref) in enumerate(((m1_ref, l1_ref), (m4_ref, l4_ref), (m16_ref, l16_ref))):
        natural_order(m_ref, stat_scr.at[2 * gi])
        natural_order(l_ref, stat_scr.at[2 * gi + 1])
        ms.append(stat_scr[2 * gi, 0])
        ls.append(stat_scr[2 * gi + 1, 0])
    mx = jnp.maximum(jnp.maximum(ms[0], ms[1]), ms[2])
    es = [jnp.exp2(m - mx) for m in ms]
    den = es[0] * ls[0] + es[1] * ls[1] + es[2] * ls[2]
    expand = e_ref[...]
    o_a = jnp.zeros(x.shape, F32)
    for e, o_ref in zip(es, (o1_ref, o4_ref, o16_ref)):
        w = e / den
        hi = w.astype(BF16)
        lo = (w - hi.astype(F32)).astype(BF16)
        natural_order(o_ref, o_scr)
        o_g = jnp.concatenate([o_scr[p] for p in range(N_PAIRS)], axis=1)
        o_a = o_a + _dot(jnp.concatenate([hi, lo], axis=1), expand) * o_g

    merged = (g_a * _dot(o_a.astype(BF16), pa_ref[...])
              + g_b * _dot(ob_ref[0], pb_ref[...]))
    out_ref[0] = x + _dot(merged.astype(BF16), wo_ref[...])


def _merge_call(x, g, wg, os_, ms, ls, ob, pa, pb, wo, tm=256):
    b, s, _ = x.shape
    expand = jnp.kron(jnp.eye(N_HEADS, dtype=F32), jnp.ones((1, HEAD_DIM), F32))
    expand = jnp.concatenate([expand, jnp.zeros((LANES - N_HEADS, D_MODEL), F32)], 0).astype(BF16)
    expand = jnp.concatenate([expand, expand], axis=0)
    const = lambda bb, i: (0, 0)
    wide = pl.BlockSpec((1, tm, D_MODEL), lambda bb, i: (bb, i, 0))
    sq = pl.BlockSpec((D_MODEL, D_MODEL), const)
    res = lambda a: pl.BlockSpec((1, a.shape[1], tm // a.shape[1], a.shape[3]),
                                 lambda bb, i: (bb, 0, i, 0))
    return pl.pallas_call(
        _merge_kernel,
        grid=(b, s // tm),
        in_specs=[wide, pl.BlockSpec((1, D_MODEL), const),
                  pl.BlockSpec((D_MODEL, 2 * D_MODEL), const),
                  *[res(a) for a in (*os_, *ms, *ls)], wide, sq, sq, sq,
                  pl.BlockSpec((2 * LANES, D_MODEL), const)],
        out_specs=wide,
        out_shape=jax.ShapeDtypeStruct((b, s, D_MODEL), F32),
        scratch_shapes=[pltpu.VMEM((N_PAIRS, tm, LANES), F32),
                        pltpu.VMEM((len(ms) + len(ls), 1, tm, LANES), F32)],
        compiler_params=pltpu.CompilerParams(dimension_semantics=("arbitrary", "arbitrary")),
        name="merge_proj",
    )(x, g, wg, *os_, *ms, *ls, ob, pa, pb, wo, expand)


def _ffn_kernel(x_ref, g_ref, wg_ref, wu_ref, wd_ref, gf_ref, out_ref, *, final_norm):
    x = x_ref[...]
    h = _rmsnorm(x, g_ref[...]).astype(BF16)
    act = (jax.nn.silu(_dot(h, wg_ref[...])) * _dot(h, wu_ref[...])).astype(BF16)
    x2 = x + _dot(act, wd_ref[...])
    out_ref[...] = _rmsnorm(x2, gf_ref[...]) if final_norm else x2


def _ffn_call(x2, g, wg, wu, wd, gf, final_norm, tm=512):
    t = x2.shape[0]
    row = lambda i: (i, 0)
    const = lambda i: (0, 0)
    vec = pl.BlockSpec((1, D_MODEL), const)
    resident = lambda shape: pl.BlockSpec(shape, const, pipeline_mode=pl.Buffered(1))
    return pl.pallas_call(
        functools.partial(_ffn_kernel, final_norm=final_norm),
        grid=(t // tm,),
        in_specs=[pl.BlockSpec((tm, D_MODEL), row), vec,
                  resident((D_MODEL, D_FF)), resident((D_MODEL, D_FF)),
                  resident((D_FF, D_MODEL)), vec],
        out_specs=pl.BlockSpec((tm, D_MODEL), row),
        out_shape=jax.ShapeDtypeStruct((t, D_MODEL), F32),
        compiler_params=pltpu.CompilerParams(dimension_semantics=("arbitrary",)),
        name="ffn_final",
    )(x2, g, wg, wu, wd, gf)


def _rotary_tables(seq):
    half = ROPE_DIM // 2
    in_head = np.arange(LANES) % HEAD_DIM
    inv_freq = ROPE_THETA ** (-(in_head % half).astype(np.float64) * (2.0 / ROPE_DIM))
    ang = np.arange(seq, dtype=np.float64)[:, None] * inv_freq[None, :]
    cos, sin = np.cos(ang), np.sin(ang)
    first = (in_head < half)[None, :]
    second = ((in_head >= half) & (in_head < ROPE_DIM))[None, :]
    c = np.where(first | second, cos, 1.0)
    s1 = np.where(second, sin, 0.0)
    s2 = np.where(first, -sin, 0.0)
    return tuple(jnp.asarray(t.astype(np.float32)) for t in (c, s1, s2))


def kernel(x, norm_mix_g, w_in, shift_mu, decay_w0, decay_w2, iclr_a0, iclr_a2, gate_g2, k_k, k_a, r_k, ln_x_w, ln_x_b, proj_attn, proj_rwkv, w_out, norm_ffn_g, ffn_w_gate, ffn_w_up, ffn_w_down, norm_final_g):
    b, s, d = x.shape
    assert d == D_MODEL and s % (max(dil for _, dil in DILATED_GROUPS) * 1024) == 0
    depth = w_in.shape[0]
    c, s1, s2 = _rotary_tables(s)
    vec = lambda a: a.reshape(1, -1).astype(F32)
    n_attn = 3 * D_MODEL
    for l in range(depth):
        g_mix = vec(norm_mix_g[l])
        w = w_in[l]
        qkv = _qkv_call(x, g_mix, w[:, :n_attn].astype(BF16), c, s1, s2)
        zero = jnp.zeros((DECAY_LORA, D_MODEL), F32)
        w2a2 = jnp.concatenate(
            [jnp.concatenate([decay_w2[l], zero], axis=1),
             jnp.concatenate([zero, iclr_a2[l]], axis=1)], axis=0).astype(BF16)
        feats = _rwkv_feat_call(x, g_mix, w[:, n_attn:n_attn + D_SHIFTED].astype(BF16),
                                vec(shift_mu[l]), vec(decay_w0[l]), w2a2, vec(iclr_a0[l]),
                                gate_g2[l].astype(BF16), vec(k_k[l]), vec(k_a[l]), vec(r_k[l]))
        os_, ms, ls = [], [], []
        for gi in range(len(DILATED_GROUPS)):
            o_g, m_g, l_g = _attn_call(*qkv[3 * gi:3 * gi + 3])
            os_.append(o_g)
            ms.append(m_g)
            ls.append(l_g)
        o_b = _rwkv_call(feats, vec(ln_x_w[l]), vec(ln_x_b[l]))
        x = _merge_call(x, g_mix, w[:, n_attn + D_SHIFTED:].astype(BF16), os_, ms, ls, o_b,
                        proj_attn[l].astype(BF16), proj_rwkv[l].astype(BF16),
                        w_out[l].astype(BF16))
        x = _ffn_call(x.reshape(b * s, d), vec(norm_ffn_g[l]), ffn_w_gate[l].astype(BF16),
                      ffn_w_up[l].astype(BF16), ffn_w_down[l].astype(BF16),
                      vec(norm_final_g), final_norm=(l == depth - 1)).reshape(b, s, d)
    return x
```
